```python
import math
import jax, jax.numpy as jnp
from jax import lax
import numpy as np

D_MODEL = 1024
BATCH = 8
SEQ = 8192
DEPTH = 1

D_MIX = D_MODEL
RWKV_HEADS = 8
RWKV_HEAD_DIM = 64
D_RWKV = RWKV_HEADS * RWKV_HEAD_DIM
DECAY_RANK = 64
ICLR_RANK = 64
GATE_RANK = 128
D_SHIFT = 3 * D_RWKV + DECAY_RANK + ICLR_RANK + GATE_RANK
S5_GROUP_CH = 16
D_S5 = D_MIX - D_RWKV
S5_GROUPS = D_S5 // S5_GROUP_CH
S5_STATE = 64
D_IN = D_SHIFT + D_S5
N_MEM = 256
XATTN_HEADS = 4
XATTN_HEAD_DIM = D_MODEL // XATTN_HEADS
N_EXPERT_GROUPS = 4
EXPERTS_PER_GROUP = 8
N_EXPERTS = N_EXPERT_GROUPS * EXPERTS_PER_GROUP
EXPERT_TOP_K = 2
D_EXPERT = 512
MOE_BLOCK = 128
RMS_EPS = 1e-6
GN_EPS = 64e-5
L2_EPS = 1e-12

kernel_name = "hybrid_rwkv7_s5_memxattn_hiermoe"


def rmsnorm(x, g):
    xf = x.astype(jnp.float32)
    y = xf * lax.rsqrt(jnp.mean(xf * xf, axis=-1, keepdims=True) + RMS_EPS)
    return (y * g.astype(jnp.float32)).astype(x.dtype)


def token_shift(z, mu):
    prev = jnp.pad(z, ((0, 0), (1, 0), (0, 0)))[:, :-1]
    return z + (prev - z) * mu


def rwkv7_group(p, w0, w_decay_up, a0, w_iclr_up, w_gate_up, k_k, k_a, r_k, ln_w, ln_b):
    bsz, slen = p.shape[0], p.shape[1]
    cuts = [D_RWKV, 2 * D_RWKV, 3 * D_RWKV, 3 * D_RWKV + DECAY_RANK, 3 * D_RWKV + DECAY_RANK + ICLR_RANK]
    r, k, v, dw, da, dg = jnp.split(p, cuts, axis=-1)
    log_w = -jax.nn.softplus(-(w0 + jnp.tanh(dw) @ w_decay_up)) - 0.5
    decay = jnp.exp(-jnp.exp(log_w))
    a = jax.nn.sigmoid(a0 + da @ w_iclr_up)
    g = jax.nn.sigmoid(dg) @ w_gate_up
    hd = lambda t: t.reshape(bsz, slen, RWKV_HEADS, RWKV_HEAD_DIM)
    kk = hd(k * k_k)
    kk = kk / jnp.maximum(jnp.sqrt(jnp.sum(kk * kk, axis=-1, keepdims=True)), L2_EPS)
    k = k * (1.0 + (a - 1.0) * k_a)
    r, k, v, decay, a = hd(r), hd(k), hd(v), hd(decay), hd(a)

    def step(state, inp):
        r_t, w_t, k_t, v_t, kk_t, a_t = inp
        sa = jnp.einsum('bhvk,bhk->bhv', state, -kk_t)
        state = (state * w_t[:, :, None, :]
                 + sa[..., None] * (kk_t * a_t)[:, :, None, :]
                 + v_t[..., None] * k_t[:, :, None, :])
        return state, jnp.einsum('bhvk,bhk->bhv', state, r_t)

    xs = tuple(jnp.moveaxis(t, 1, 0) for t in (r, decay, k, v, kk, a))
    state0 = jnp.zeros((bsz, RWKV_HEADS, RWKV_HEAD_DIM, RWKV_HEAD_DIM), jnp.float32)
    _, y = lax.scan(step, state0, xs)
    y = jnp.moveaxis(y, 0, 1)
    mu = jnp.mean(y, axis=-1, keepdims=True)
    var = jnp.mean(jnp.square(y - mu), axis=-1, keepdims=True)
    y = (y - mu) * lax.rsqrt(var + GN_EPS)
    y = y * ln_w.reshape(RWKV_HEADS, RWKV_HEAD_DIM) + ln_b.reshape(RWKV_HEADS, RWKV_HEAD_DIM)
    y = y + jnp.sum(r * k * r_k, axis=-1, keepdims=True) * v
    return y.reshape(bsz, slen, D_RWKV) * g


def _ssm_combine(e1, e2):
    a1, b1 = e1
    a2, b2 = e2
    return a2 * a1, a2 * b1 + b2


def s5_group(u, lam_re, lam_im, log_dt, b_re, b_im, c_re, c_im, d_skip, w_glu, b_glu):
    bsz, slen = u.shape[0], u.shape[1]
    ug = u.reshape(bsz, slen, S5_GROUPS, S5_GROUP_CH)
    lam = lax.complex(lam_re.astype(jnp.float32), lam_im.astype(jnp.float32))
    dt = jnp.exp(log_dt.astype(jnp.float32))[:, None]
    lam_bar = jnp.exp(lam * dt)
    b = lax.complex(b_re.astype(jnp.float32), b_im.astype(jnp.float32))
    b_bar = ((lam_bar - 1.0) / lam)[..., None] * b
    bu = lax.complex(jnp.einsum('bsgh,gnh->bsgn', ug, jnp.real(b_bar)),
                     jnp.einsum('bsgh,gnh->bsgn', ug, jnp.imag(b_bar)))
    a_elems = jnp.broadcast_to(lam_bar, (1, slen, S5_GROUPS, S5_STATE))
    _, states = lax.associative_scan(_ssm_combine, (a_elems, bu), axis=1)
    y = (jnp.einsum('bsgn,ghn->bsgh', jnp.real(states), c_re)
         - jnp.einsum('bsgn,ghn->bsgh', jnp.imag(states), c_im)
         + d_skip * ug)
    z = jax.nn.gelu(y.reshape(bsz, slen, D_S5))
    return z * jax.nn.sigmoid(z @ w_glu + b_glu)


def memory_cross_attention(h, mem_n, w_q, w_k, w_v, w_o):
    bsz, slen = h.shape[0], h.shape[1]
    q = (h @ w_q).reshape(bsz, slen, XATTN_HEADS, XATTN_HEAD_DIM)
    k = (mem_n @ w_k).reshape(bsz, N_MEM, XATTN_HEADS, XATTN_HEAD_DIM)
    v = (mem_n @ w_v).reshape(bsz, N_MEM, XATTN_HEADS, XATTN_HEAD_DIM)
    s = jnp.einsum('bshd,bmhd->bhsm', q, k).astype(jnp.float32) * (XATTN_HEAD_DIM ** -0.5)
    pr = jax.nn.softmax(s, axis=-1).astype(v.dtype)
    o = jnp.einsum('bhsm,bmhd->bshd', pr, v).reshape(bsz, slen, D_MODEL)
    return o @ w_o


def hierarchical_moe(h, w_rg, b_rg, w_re, b_re, w_gate, w_up, w_down):
    bsz, slen = h.shape[0], h.shape[1]
    hf = h.reshape(-1, D_MODEL)
    n_tok = hf.shape[0]
    logits1 = (hf @ w_rg).astype(jnp.float32) + b_rg
    grp = jnp.argmax(logits1, axis=-1)
    g1 = jnp.take_along_axis(jax.nn.softmax(logits1, axis=-1), grp[:, None], axis=-1)
    logits2 = jnp.einsum('nd,gde->nge', hf, w_re).astype(jnp.float32) + b_re
    logits2 = jnp.take_along_axis(logits2, grp[:, None, None], axis=1)[:, 0]
    top_v, top_i = lax.top_k(logits2, EXPERT_TOP_K)
    gates = g1 * jax.nn.softmax(top_v, axis=-1)
    eid = grp[:, None] * EXPERTS_PER_GROUP + top_i
    n_assign = n_tok * EXPERT_TOP_K
    n_blocks = -(-n_assign // MOE_BLOCK) + N_EXPERTS
    n_slots = n_blocks * MOE_BLOCK
    flat_e = eid.reshape(-1)
    flat_g = gates.reshape(-1)
    flat_tok = jnp.repeat(jnp.arange(n_tok, dtype=jnp.int32), EXPERT_TOP_K)
    order = jnp.argsort(flat_e)
    sorted_e = flat_e[order]
    counts = jnp.bincount(flat_e, length=N_EXPERTS)
    padded = (counts + MOE_BLOCK - 1) // MOE_BLOCK * MOE_BLOCK
    pad_end = jnp.cumsum(padded)
    pad_start = pad_end - padded
    start = jnp.cumsum(counts) - counts
    dest = pad_start[sorted_e] + jnp.arange(n_assign) - start[sorted_e]
    slot_tok = jnp.zeros((n_slots,), jnp.int32).at[dest].set(flat_tok[order])
    slot_gate = jnp.zeros((n_slots,), jnp.float32).at[dest].set(flat_g[order])
    block_start = jnp.arange(n_blocks) * MOE_BLOCK
    block_expert = jnp.minimum(jnp.searchsorted(pad_end, block_start, side='right'), N_EXPERTS - 1)

    def run_block(args):
        tok, gate, e = args
        xb = hf[tok]
        hid = jax.nn.silu(xb @ w_gate[e]) * (xb @ w_up[e])
        return (hid @ w_down[e]) * gate[:, None].astype(xb.dtype)

    y = lax.map(run_block, (slot_tok.reshape(n_blocks, MOE_BLOCK),
                            slot_gate.reshape(n_blocks, MOE_BLOCK), block_expert))
    out = jnp.zeros_like(hf).at[slot_tok].add(y.reshape(n_slots, D_MODEL).astype(hf.dtype))
    return out.reshape(bsz, slen, D_MODEL)


def setup_inputs(seed: int = 0) -> dict:
    key = jax.random.key(seed)
    ks = iter(jax.random.split(key, 48))
    L = DEPTH
    f32 = jnp.float32
    nrm = lambda shape, scale: jax.random.normal(next(ks), shape, f32) * scale
    uni = lambda shape, lo, hi: jax.random.uniform(next(ks), shape, f32, lo, hi)
    gain = lambda shape: 1.0 + nrm(shape, 0.02)
    return {
        "x": nrm((BATCH, SEQ, D_MODEL), 1.0),
        "mem": nrm((BATCH, N_MEM, D_MODEL), 1.0),
        "norm_mix_w": gain((L, D_MODEL)),
        "w_in": nrm((L, D_MODEL, D_IN), D_MODEL ** -0.5),
        "mu_shift": uni((L, D_SHIFT), 0.0, 1.0),
        "w0": uni((L, D_RWKV), -6.0, 1.0),
        "w_decay_up": nrm((L, DECAY_RANK, D_RWKV), 0.5 * DECAY_RANK ** -0.5),
        "a0": uni((L, D_RWKV), -0.5, 0.5),
        "w_iclr_up": nrm((L, ICLR_RANK, D_RWKV), 0.5 * ICLR_RANK ** -0.5),
        "w_gate_up": nrm((L, GATE_RANK, D_RWKV), GATE_RANK ** -0.5),
        "k_k": 0.85 + nrm((L, D_RWKV), 0.02),
        "k_a": gain((L, D_RWKV)),
        "r_k": nrm((L, RWKV_HEADS, RWKV_HEAD_DIM), 0.1),
        "ln_x_w": gain((L, D_RWKV)),
        "ln_x_b": nrm((L, D_RWKV), 0.01),
        "s5_lam_re": -0.5 + nrm((L, S5_GROUPS, S5_STATE), 0.01),
        "s5_lam_im": jnp.pi * jnp.arange(S5_STATE, dtype=f32) + nrm((L, S5_GROUPS, S5_STATE), 0.01),
        "s5_log_dt": uni((L, S5_GROUPS), math.log(1e-3), math.log(1e-1)),
        "s5_b_re": nrm((L, S5_GROUPS, S5_STATE, S5_GROUP_CH), (2 * S5_GROUP_CH) ** -0.5),
        "s5_b_im": nrm((L, S5_GROUPS, S5_STATE, S5_GROUP_CH), (2 * S5_GROUP_CH) ** -0.5),
        "s5_c_re": nrm((L, S5_GROUPS, S5_GROUP_CH, S5_STATE), S5_STATE ** -0.5),
        "s5_c_im": nrm((L, S5_GROUPS, S5_GROUP_CH, S5_STATE), S5_STATE ** -0.5),
        "s5_d": nrm((L, S5_GROUPS, S5_GROUP_CH), 1.0),
        "w_glu": nrm((L, D_S5, D_S5), D_S5 ** -0.5),
        "b_glu": nrm((L, D_S5), 0.01),
        "w_mix_out": nrm((L, D_MIX, D_MODEL), D_MIX ** -0.5),
        "norm_xattn_w": gain((L, D_MODEL)),
        "norm_mem_w": gain((L, D_MODEL)),
        "w_q": nrm((L, D_MODEL, D_MODEL), D_MODEL ** -0.5),
        "w_k": nrm((L, D_MODEL, D_MODEL), D_MODEL ** -0.5),
        "w_v": nrm((L, D_MODEL, D_MODEL), D_MODEL ** -0.5),
        "w_o": nrm((L, D_MODEL, D_MODEL), D_MODEL ** -0.5),
        "norm_moe_w": gain((L, D_MODEL)),
        "w_router_group": nrm((L, D_MODEL, N_EXPERT_GROUPS), D_MODEL ** -0.5),
        "b_router_group": nrm((L, N_EXPERT_GROUPS), 0.01),
        "w_router_expert": nrm((L, N_EXPERT_GROUPS, D_MODEL, EXPERTS_PER_GROUP), D_MODEL ** -0.5),
        "b_router_expert": nrm((L, N_EXPERT_GROUPS, EXPERTS_PER_GROUP), 0.01),
        "w_exp_gate": nrm((L, N_EXPERTS, D_MODEL, D_EXPERT), D_MODEL ** -0.5),
        "w_exp_up": nrm((L, N_EXPERTS, D_MODEL, D_EXPERT), D_MODEL ** -0.5),
        "w_exp_down": nrm((L, N_EXPERTS, D_EXPERT, D_MODEL), D_EXPERT ** -0.5),
        "norm_final_w": gain((D_MODEL,)),
    }


def reference(x, mem, norm_mix_w, w_in, mu_shift, w0, w_decay_up, a0, w_iclr_up, w_gate_up, k_k, k_a,
              r_k, ln_x_w, ln_x_b, s5_lam_re, s5_lam_im, s5_log_dt, s5_b_re, s5_b_im, s5_c_re, s5_c_im,
              s5_d, w_glu, b_glu, w_mix_out, norm_xattn_w, norm_mem_w, w_q, w_k, w_v, w_o, norm_moe_w,
              w_router_group, b_router_group, w_router_expert, b_router_expert, w_exp_gate, w_exp_up,
              w_exp_down, norm_final_w):
    for l in range(DEPTH):
        h = rmsnorm(x, norm_mix_w[l])
        proj = (h @ w_in[l]).astype(jnp.float32)
        p_rwkv = token_shift(proj[..., :D_SHIFT], mu_shift[l])
        u_s5 = proj[..., D_SHIFT:]
        y_rwkv = rwkv7_group(p_rwkv, w0[l], w_decay_up[l], a0[l], w_iclr_up[l], w_gate_up[l],
                             k_k[l], k_a[l], r_k[l], ln_x_w[l], ln_x_b[l])
        y_s5 = s5_group(u_s5, s5_lam_re[l], s5_lam_im[l], s5_log_dt[l], s5_b_re[l], s5_b_im[l],
                        s5_c_re[l], s5_c_im[l], s5_d[l], w_glu[l], b_glu[l])
        y_mix = jnp.concatenate([y_rwkv, y_s5], axis=-1).astype(x.dtype)
        x = x + y_mix @ w_mix_out[l]
        h = rmsnorm(x, norm_xattn_w[l])
        mem_n = rmsnorm(mem, norm_mem_w[l])
        x = x + memory_cross_attention(h, mem_n, w_q[l], w_k[l], w_v[l], w_o[l])
        h = rmsnorm(x, norm_moe_w[l])
        x = x + hierarchical_moe(h, w_router_group[l], b_router_group[l], w_router_expert[l],
                                 b_router_expert[l], w_exp_gate[l], w_exp_up[l], w_exp_down[l])
    return rmsnorm(x, norm_final_w)
```

```python
import functools
import math

import jax
import jax.numpy as jnp
from jax import lax
from jax.experimental import pallas as pl
from jax.experimental.pallas import tpu as pltpu

F32 = jnp.float32
BF16 = jnp.bfloat16
I32 = jnp.int32

HEAD_DIM = 64
D_RWKV = 512
DECAY_RANK = 64
ICLR_RANK = 64
GATE_RANK = 128
D_SHIFT = 3 * D_RWKV + DECAY_RANK + ICLR_RANK + GATE_RANK
S5_CH = 16
S5_STATE = 64
D_S5 = 512
S5_GROUPS = D_S5 // S5_CH
XATTN_HEADS = 4
N_EXPERT_GROUPS = 4
EXPERTS_PER_GROUP = 8
N_EXPERTS = N_EXPERT_GROUPS * EXPERTS_PER_GROUP
TOP_K = 2
RMS_EPS = 1e-6
GN_EPS = 64e-5
L2_EPS = 1e-12

LANES = 128
SUBLANES = 8
VMEM_LIMIT_BYTES = 56 * 1024 * 1024

RWKV_CHUNK = 64
PAIR = 2 * HEAD_DIM
S5_CHUNK = 16
S5_PAIR_W = 2 * S5_CHUNK * S5_CH
MOE_BLOCK_ROWS = 256
ROUTER_LANES = 128


def _dot(a, b):
    return jnp.dot(a, b, preferred_element_type=F32)


def _dot_nt(a, b):
    return lax.dot_general(a, b, (((1,), (1,)), ((), ())), preferred_element_type=F32)


def _dot_tn(a, b):
    return lax.dot_general(a, b, (((0,), (0,)), ((), ())), preferred_element_type=F32)


def _split2(x):
    hi = x.astype(BF16)
    lo = (x - hi.astype(F32)).astype(BF16)
    return hi, lo


def _split3(x):
    hi = x.astype(BF16)
    r = x - hi.astype(F32)
    mid = r.astype(BF16)
    lo = (r - mid.astype(F32)).astype(BF16)
    return hi, mid, lo


def _stack3(w):
    hi, lo = _split2(w.astype(F32))
    return jnp.concatenate([hi, hi, lo], axis=0)


def _dot3(x, w3):
    hi, lo = _split2(x)
    return _dot(jnp.concatenate([hi, lo, hi], axis=1), w3)


def _rms(x, w):
    return x * lax.rsqrt(jnp.mean(x * x, axis=-1, keepdims=True) + RMS_EPS) * w


def _softplus(x):
    return jnp.maximum(x, 0.0) + jnp.log1p(jnp.exp(-jnp.abs(x)))


def _params(*sem):
    return pltpu.CompilerParams(dimension_semantics=sem, vmem_limit_bytes=VMEM_LIMIT_BYTES)


def _inproj_kernel(x_ref, nw_ref, w_ref, mu_ref, p_ref, u_ref, carry_ref):
    tm = x_ref.shape[1]

    @pl.when(pl.program_id(1) == 0)
    def _():
        carry_ref[...] = jnp.zeros_like(carry_ref)

    h = _rms(x_ref[0], nw_ref[...]).astype(BF16)
    proj = _dot(h, w_ref[...])
    z = proj[:, :D_SHIFT]
    row = lax.broadcasted_iota(I32, z.shape, 0)
    prev = jnp.where(row == 0, carry_ref[0:1, :], pltpu.roll(z, 1, 0))
    carry_ref[0:1, :] = z[tm - 1:tm, :]
    p_ref[0] = z + (prev - z) * mu_ref[...]
    u_ref[0] = proj[:, D_SHIFT:].astype(BF16)


def _inproj(x, norm_w, w_in, mu, tm):
    b, s, d = x.shape
    d_in = w_in.shape[1]
    return pl.pallas_call(
        _inproj_kernel,
        grid=(b, s // tm),
        in_specs=[
            pl.BlockSpec((1, tm, d), lambda i, j: (i, j, 0)),
            pl.BlockSpec((1, d), lambda i, j: (0, 0)),
            pl.BlockSpec((d, d_in), lambda i, j: (0, 0)),
            pl.BlockSpec((1, D_SHIFT), lambda i, j: (0, 0)),
        ],
        out_specs=[
            pl.BlockSpec((1, tm, D_SHIFT), lambda i, j: (i, j, 0)),
            pl.BlockSpec((1, tm, d_in - D_SHIFT), lambda i, j: (i, j, 0)),
        ],
        out_shape=[
            jax.ShapeDtypeStruct((b, s, D_SHIFT), F32),
            jax.ShapeDtypeStruct((b, s, d_in - D_SHIFT), BF16),
        ],
        scratch_shapes=[pltpu.VMEM((SUBLANES, D_SHIFT), F32)],
        compiler_params=_params("arbitrary", "arbitrary"),
        name="inproj",
    )(x, norm_w.reshape(1, d), w_in.astype(BF16), mu.reshape(1, D_SHIFT))


def _headsum(x, ones2):
    hi, lo = _split2(x)
    return _dot(jnp.concatenate([hi, lo], axis=1), ones2)


def _rwkv_kernel(p_ref, wcat_ref, wg_ref, vec_ref, tri_ref, ones2_ref, y_ref,
                 lw_s, cum_s, a_s, g_s, h_s):
    t2 = p_ref.shape[1]
    c = RWKV_CHUNK
    n_pairs = D_RWKV // PAIR

    @pl.when(pl.program_id(1) == 0)
    def _():
        h_s[...] = jnp.zeros_like(h_s)

    lane_t = lax.broadcasted_iota(I32, (t2, LANES), 1)
    lo_col = 3 * D_RWKV
    x = p_ref[0, :, lo_col:lo_col + LANES]
    zz = _dot3(jnp.where(lane_t < DECAY_RANK, jnp.tanh(x), x), wcat_ref[...])
    lw = -jnp.exp(-_softplus(-(vec_ref[0:1, :] + zz[:, :D_RWKV])) - 0.5)
    lw_s[...] = lw
    hi, mid, lo = _split3(lw)
    c3 = _dot(tri_ref[...], jnp.concatenate([hi, mid, lo], axis=1))
    cum_s[...] = c3[:, :D_RWKV] + c3[:, D_RWKV:2 * D_RWKV] + c3[:, 2 * D_RWKV:]
    a_s[...] = jax.nn.sigmoid(vec_ref[1:2, :] + zz[:, D_RWKV:])
    dg = p_ref[0, :, lo_col + LANES:lo_col + 2 * LANES]
    g_s[...] = _dot(jax.nn.sigmoid(dg).astype(BF16), wg_ref[...])

    row = lax.broadcasted_iota(I32, (PAIR, PAIR), 0)
    col = lax.broadcasted_iota(I32, (PAIR, PAIR), 1)
    same_head = (row >= c) == (col >= c)
    mask_strict = same_head & (col < row)
    mask_incl = same_head & (col <= row)
    eye = row == col
    head0 = lax.broadcasted_iota(I32, (c, PAIR), 1) < HEAD_DIM
    ones2 = ones2_ref[...]

    def stack(v):
        return jnp.concatenate([jnp.where(head0, v, 0.0), jnp.where(head0, 0.0, v)], axis=0)

    def chunk(ci, carry):
        rows = pl.ds(pl.multiple_of(ci * c, c), c)
        for j in range(n_pairs):
            ls = slice(j * PAIR, (j + 1) * PAIR)
            r = p_ref[0, rows, j * PAIR:(j + 1) * PAIR]
            k = p_ref[0, rows, D_RWKV + j * PAIR:D_RWKV + (j + 1) * PAIR]
            v = p_ref[0, rows, 2 * D_RWKV + j * PAIR:2 * D_RWKV + (j + 1) * PAIR]
            lw_c = lw_s[rows, ls]
            cum = cum_s[rows, ls]
            a = a_s[rows, ls]
            k_k = vec_ref[2:3, ls]
            k_a = vec_ref[3:4, ls]
            r_k = vec_ref[4:5, ls]
            ln_w = vec_ref[5:6, ls]
            ln_b = vec_ref[6:7, ls]

            kk = k * k_k
            kkn = kk / jnp.maximum(jnp.sqrt(_headsum(kk * kk, ones2)), L2_EPS)
            km = k * (1.0 + (a - 1.0) * k_a)
            beta = kkn * a
            cum_end = cum[c - 1:c, :]
            e_neg = jnp.exp(-cum)
            e_end = jnp.exp(cum_end - cum)
            a_t = stack(-kkn * jnp.exp(cum - lw_c))
            r_t = stack(r * jnp.exp(cum))
            b_t = beta * e_neg
            k_t = km * e_neg
            v_s = stack(v)
            bk_end = jnp.concatenate([stack(beta * e_end), stack(km * e_end)], axis=0).astype(BF16)
            w_end = jnp.exp(cum_end)

            lhs = jnp.concatenate([a_t, r_t], axis=0).astype(BF16)
            rhs = jnp.concatenate([b_t, b_t, k_t, k_t], axis=0).astype(BF16)
            gram = _dot_nt(lhs, rhs)
            l_ab = jnp.where(mask_strict, gram[:PAIR, :PAIR], 0.0)
            l_ak = jnp.where(mask_strict, gram[:PAIR, PAIR:], 0.0)
            m_rb = jnp.where(mask_incl, gram[PAIR:, :PAIR], 0.0)
            m_rk = jnp.where(mask_incl, gram[PAIR:, PAIR:], 0.0)

            inv = jnp.where(eye, 1.0, l_ab)
            lp = l_ab.astype(BF16)
            n = 1
            while 2 * n < c:
                lp = _dot(lp, lp).astype(BF16)
                inv = inv + _dot(inv.astype(BF16), lp)
                n *= 2

            s0 = h_s[j]
            xs = _dot_nt(lhs, s0.astype(BF16))
            w1 = _dot(l_ak.astype(BF16), v_s.astype(BF16))
            u = _dot(inv.astype(BF16), (xs[:PAIR] + w1).astype(BF16))
            uv = jnp.concatenate([u, v_s], axis=0).astype(BF16)
            ys = xs[PAIR:] + _dot(jnp.concatenate([m_rb, m_rk], axis=1).astype(BF16), uv)
            y = ys[:c] + ys[c:]
            h_s[j] = s0 * w_end + _dot_tn(uv, bk_end)

            mu = _headsum(y, ones2) * (1.0 / HEAD_DIM)
            d = y - mu
            var = _headsum(d * d, ones2) * (1.0 / HEAD_DIM)
            yn = d * lax.rsqrt(var + GN_EPS) * ln_w + ln_b
            bonus = _headsum(r * km * r_k, ones2)
            y_ref[0, rows, ls] = ((yn + bonus * v) * g_s[rows, ls]).astype(BF16)
        return carry

    lax.fori_loop(0, t2 // c, chunk, 0)


def _rwkv(p, w0, w_decay_up, a0, w_iclr_up, w_gate_up, k_k, k_a, r_k, ln_w, ln_b, t2):
    b, s, _ = p.shape
    zeros = jnp.zeros((DECAY_RANK, D_RWKV), F32)
    wcat = jnp.concatenate([jnp.concatenate([w_decay_up, zeros], axis=1),
                            jnp.concatenate([zeros, w_iclr_up], axis=1)], axis=0)
    vec = jnp.stack([w0, a0, k_k, k_a, r_k.reshape(-1), ln_w, ln_b, jnp.zeros_like(w0)])
    ti = jnp.arange(t2)
    tri = ((ti[:, None] // RWKV_CHUNK == ti[None, :] // RWKV_CHUNK) & (ti[None, :] <= ti[:, None])).astype(BF16)
    li = jnp.arange(PAIR)
    ones_bd = (li[:, None] // HEAD_DIM == li[None, :] // HEAD_DIM).astype(BF16)
    ones2 = jnp.concatenate([ones_bd, ones_bd], axis=0)
    const = lambda shape: pl.BlockSpec(shape, lambda i, j: (0,) * len(shape))
    return pl.pallas_call(
        _rwkv_kernel,
        grid=(b, s // t2),
        in_specs=[
            pl.BlockSpec((1, t2, D_SHIFT), lambda i, j: (i, j, 0)),
            const((3 * LANES, 2 * D_RWKV)),
            const((GATE_RANK, D_RWKV)),
            const((SUBLANES, D_RWKV)),
            const((t2, t2)),
            const((2 * PAIR, PAIR)),
        ],
        out_specs=pl.BlockSpec((1, t2, D_RWKV), lambda i, j: (i, j, 0)),
        out_shape=jax.ShapeDtypeStruct((b, s, D_RWKV), BF16),
        scratch_shapes=[pltpu.VMEM((t2, D_RWKV), F32)] * 4
        + [pltpu.VMEM((D_RWKV // PAIR, PAIR, PAIR), F32)],
        compiler_params=_params("arbitrary", "arbitrary"),
        name="rwkv",
    )(p, _stack3(wcat), w_gate_up.astype(BF16), vec, tri, ones2)


def _s5_kernel(u_ref, m_ref, pre_ref, pim_ref, qre_ref, qim_ref, lam_ref, y_ref, xr_s, xi_s):
    nb = SUBLANES
    u = u_ref[0]
    xr_s[...] = _dot(u, pre_ref[0])
    xi_s[...] = _dot(u, pim_ref[0])
    lr = jnp.broadcast_to(lam_ref[0, 0:1, :], (nb, LANES))
    li = jnp.broadcast_to(lam_ref[0, 1:2, :], (nb, LANES))

    def step(ci, carry):
        xr, xi = carry
        rows = pl.ds(pl.multiple_of(ci * nb, nb), nb)
        ar = xr_s[rows, :]
        ai = xi_s[rows, :]
        xr_s[rows, :] = xr
        xi_s[rows, :] = xi
        return lr * xr - li * xi + ar, lr * xi + li * xr + ai

    zero = jnp.zeros((nb, LANES), F32)
    lax.fori_loop(0, u.shape[0] // nb, step, (zero, zero), unroll=8)
    y_ref[0] = (_dot(u, m_ref[0]) + _dot(xr_s[...].astype(BF16), qre_ref[0])
                + _dot(xi_s[...].astype(BF16), qim_ref[0]))


def _s5_tables(lam_re, lam_im, log_dt, b_re, b_im, c_re, c_im, d_skip):
    tc = S5_CHUNK
    lam = lax.complex(lam_re, lam_im)
    dt = jnp.exp(log_dt)[:, None]
    lam_bar = jnp.exp(lam * dt)
    b_bar = ((lam_bar - 1.0) / lam)[..., None] * lax.complex(b_re, b_im)
    c = lax.complex(c_re, c_im)
    taus = jnp.arange(tc + 1, dtype=F32)
    pw = jnp.exp((lam * dt)[None] * taus[:, None, None])
    kern = jnp.real(jnp.einsum('gjn,tgn,gni->tgji', c, pw[:tc], b_bar))
    kern = kern.at[0].add(jax.vmap(jnp.diag)(d_skip))
    lag = jnp.arange(tc)[None, :] - jnp.arange(tc)[:, None]
    kst = jnp.where((lag >= 0)[:, :, None, None, None], kern[jnp.clip(lag, 0)], 0.0)
    m = jnp.transpose(kst, (2, 0, 4, 1, 3)).reshape(S5_GROUPS, tc * S5_CH, tc * S5_CH)
    pc = pw[tc - 1 - jnp.arange(tc)][..., None] * b_bar[None]
    pc = jnp.transpose(pc, (1, 0, 3, 2)).reshape(S5_GROUPS, tc * S5_CH, S5_STATE)
    qc = c[None] * pw[1 + jnp.arange(tc)][:, :, None, :]
    qc = jnp.transpose(qc, (1, 3, 0, 2)).reshape(S5_GROUPS, S5_STATE, tc * S5_CH)
    eye2 = jnp.eye(2, dtype=F32)

    def pair_rows(t):
        g, a, n = t.shape
        return jnp.einsum('pgan,gh->pgahn', t.reshape(g // 2, 2, a, n), eye2).reshape(g // 2, 2 * a, 2 * n)

    lam_t = pw[tc].reshape(S5_GROUPS // 2, 1, 2 * S5_STATE)
    lam2 = jnp.concatenate([jnp.real(lam_t), jnp.imag(lam_t)], axis=1)
    return (pair_rows(m).astype(BF16), pair_rows(jnp.real(pc)).astype(BF16),
            pair_rows(jnp.imag(pc)).astype(BF16), pair_rows(jnp.real(qc)).astype(BF16),
            pair_rows(-jnp.imag(qc)).astype(BF16), lam2)


def _s5(u, tables):
    b, s, _ = u.shape
    nc = s // S5_CHUNK
    npairs = S5_GROUPS // 2
    rows = nc * b
    ug = u.reshape(b, nc, S5_CHUNK, npairs, 2, S5_CH)
    ug = jnp.transpose(ug, (3, 1, 0, 4, 2, 5)).reshape(npairs, rows, S5_PAIR_W)
    m2, pre, pim, qre, qim, lam2 = tables
    per = lambda a, c_: pl.BlockSpec((1, a, c_), lambda i: (i, 0, 0))
    y = pl.pallas_call(
        _s5_kernel,
        grid=(npairs,),
        in_specs=[per(rows, S5_PAIR_W), per(S5_PAIR_W, S5_PAIR_W), per(S5_PAIR_W, LANES),
                  per(S5_PAIR_W, LANES), per(LANES, S5_PAIR_W), per(LANES, S5_PAIR_W), per(2, LANES)],
        out_specs=per(rows, S5_PAIR_W),
        out_shape=jax.ShapeDtypeStruct((npairs, rows, S5_PAIR_W), F32),
        scratch_shapes=[pltpu.VMEM((rows, LANES), F32)] * 2,
        compiler_params=_params("arbitrary"),
        name="s5",
    )(ug, m2, pre, pim, qre, qim, lam2)
    y = y.reshape(npairs, nc, b, 2, S5_CHUNK, S5_CH)
    return jnp.transpose(y, (2, 1, 4, 0, 3, 5)).reshape(b, s, D_S5)


def _memkv_kernel(mem_ref, nw_ref, wk_ref, wv_ref, k_ref, v_ref):
    m = _rms(mem_ref[0], nw_ref[...]).astype(BF16)
    k_ref[0] = _dot(m, wk_ref[...]).astype(BF16)
    v_ref[0] = _dot(m, wv_ref[...]).astype(BF16)


def _memkv(mem, norm_w, w_k, w_v):
    b, nm, d = mem.shape
    const = lambda shape: pl.BlockSpec(shape, lambda i: (0,) * len(shape))
    blk = pl.BlockSpec((1, nm, d), lambda i: (i, 0, 0))
    return pl.pallas_call(
        _memkv_kernel,
        grid=(b,),
        in_specs=[blk, const((1, d)), const((d, d)), const((d, d))],
        out_specs=[blk, blk],
        out_shape=[jax.ShapeDtypeStruct((b, nm, d), BF16)] * 2,
        compiler_params=_params("arbitrary"),
        name="memkv",
    )(mem, norm_w.reshape(1, d), w_k.astype(BF16), w_v.astype(BF16))


def _post_kernel(x_ref, yr_ref, y5_ref, km_ref, vm_ref, wglu_ref, bglu_ref, wmo_ref, nx_ref,
                 wq_ref, wo_ref, nm_ref, wr_ref, br_ref, x2_ref, h3_ref, eid_ref, gate_ref):
    d = x_ref.shape[2]
    hd = d // XATTN_HEADS
    y = y5_ref[0]
    z = 0.5 * y * (1.0 + jnp.tanh(math.sqrt(2.0 / math.pi) * (y + 0.044715 * (y * y * y))))
    glu = jax.nn.sigmoid(_dot(z.astype(BF16), wglu_ref[...]) + bglu_ref[...])
    x1 = (x_ref[0] + _dot(yr_ref[0], wmo_ref[:D_RWKV, :])
          + _dot((z * glu).astype(BF16), wmo_ref[D_RWKV:, :]))

    h = _rms(x1, nx_ref[...]).astype(BF16)
    q = (_dot(h, wq_ref[...]) * (hd ** -0.5)).astype(BF16)
    heads = []
    for i in range(XATTN_HEADS):
        sl = slice(i * hd, (i + 1) * hd)
        sc = _dot_nt(q[:, sl], km_ref[0, :, sl])
        e = jnp.exp(sc - jnp.max(sc, axis=-1, keepdims=True))
        o = _dot(e.astype(BF16), vm_ref[0, :, sl]) / jnp.sum(e, axis=-1, keepdims=True)
        heads.append(o.astype(BF16))
    x2 = x1 + _dot(jnp.concatenate(heads, axis=1), wo_ref[...])
    x2_ref[0] = x2
    h3 = _rms(x2, nm_ref[...])
    h3_ref[0] = h3

    lg = _dot3(h3, wr_ref[...]) + br_ref[...]
    lane = lax.broadcasted_iota(I32, lg.shape, 1)
    lanef = lane.astype(F32)
    neg = -jnp.inf
    no_lane = float(ROUTER_LANES)
    l1 = jnp.where(lane < N_EXPERT_GROUPS, lg, neg)
    m1 = jnp.max(l1, axis=-1, keepdims=True)
    grp = jnp.min(jnp.where(l1 == m1, lanef, no_lane), axis=-1, keepdims=True)
    g1 = 1.0 / jnp.sum(jnp.exp(l1 - m1), axis=-1, keepdims=True)
    first = N_EXPERT_GROUPS + EXPERTS_PER_GROUP * grp
    l2 = jnp.where((lanef >= first) & (lanef < first + EXPERTS_PER_GROUP), lg, neg)
    v1 = jnp.max(l2, axis=-1, keepdims=True)
    i1 = jnp.min(jnp.where(l2 == v1, lanef, no_lane), axis=-1, keepdims=True)
    l2 = jnp.where(lanef == i1, neg, l2)
    v2 = jnp.max(l2, axis=-1, keepdims=True)
    i2 = jnp.min(jnp.where(l2 == v2, lanef, no_lane), axis=-1, keepdims=True)
    e2 = jnp.exp(v2 - v1)
    den = 1.0 + e2
    eid_ref[0] = jnp.where(lane == 0, i1 - N_EXPERT_GROUPS,
                           jnp.where(lane == 1, i2 - N_EXPERT_GROUPS, 0.0)).astype(I32)
    gate_ref[0] = jnp.where(lane == 0, g1 / den, jnp.where(lane == 1, g1 * e2 / den, 0.0))


def _post(x, y_rwkv, y_s5, kmem, vmem, w_glu, b_glu, w_mix_out, norm_x, w_q, w_o, norm_moe,
          w_rg, b_rg, w_re, b_re, tm):
    b, s, d = x.shape
    nm = kmem.shape[1]
    n_logits = N_EXPERT_GROUPS + N_EXPERTS
    w_router = jnp.concatenate(
        [w_rg, jnp.transpose(w_re, (1, 0, 2)).reshape(d, N_EXPERTS),
         jnp.zeros((d, ROUTER_LANES - n_logits), F32)], axis=1)
    b_router = jnp.concatenate([b_rg, b_re.reshape(-1), jnp.zeros((ROUTER_LANES - n_logits,), F32)])
    const = lambda shape: pl.BlockSpec(shape, lambda i, j: (0,) * len(shape))
    tok = lambda w: pl.BlockSpec((1, tm, w), lambda i, j: (i, j, 0))
    mem = pl.BlockSpec((1, nm, d), lambda i, j: (i, 0, 0))
    return pl.pallas_call(
        _post_kernel,
        grid=(b, s // tm),
        in_specs=[tok(d), tok(D_RWKV), tok(D_S5), mem, mem,
                  const((D_S5, D_S5)), const((1, D_S5)), const((D_RWKV + D_S5, d)), const((1, d)),
                  const((d, d)), const((d, d)), const((1, d)),
                  const((3 * d, ROUTER_LANES)), const((1, ROUTER_LANES))],
        out_specs=[tok(d), tok(d), tok(ROUTER_LANES), tok(ROUTER_LANES)],
        out_shape=[jax.ShapeDtypeStruct((b, s, d), F32), jax.ShapeDtypeStruct((b, s, d), F32),
                   jax.ShapeDtypeStruct((b, s, ROUTER_LANES), I32),
                   jax.ShapeDtypeStruct((b, s, ROUTER_LANES), F32)],
        compiler_params=_params("arbitrary", "arbitrary"),
        name="post",
    )(x, y_rwkv, y_s5, kmem, vmem, w_glu.astype(BF16), b_glu.reshape(1, D_S5),
      w_mix_out.astype(BF16), norm_x.reshape(1, d), w_q.astype(BF16), w_o.astype(BF16),
      norm_moe.reshape(1, d), _stack3(w_router), b_router.reshape(1, ROUTER_LANES))


def _gather_pipeline(step, n_steps, idx_hbm, idx_s, src_hbm, buf, isem, rsem):
    n_rows = buf.shape[1]
    slot = step % 2

    def idx_copy(t, s):
        return pltpu.make_async_copy(idx_hbm.at[t], idx_s.at[s], isem.at[s])

    def row_copy(src_row, s, r):
        return pltpu.make_async_copy(src_hbm.at[pl.ds(src_row, 1)], buf.at[s, pl.ds(r, 1)], rsem.at[s])

    def start_rows(s):
        def body(r, c):
            row_copy(idx_s[s, r], s, r).start()
            return c
        lax.fori_loop(0, n_rows, body, 0, unroll=8)

    @pl.when(step == 0)
    def _():
        idx_copy(0, 0).start()
        idx_copy(0, 0).wait()
        start_rows(0)

        @pl.when(n_steps > 1)
        def _():
            idx_copy(1, 1).start()

    @pl.when(step + 1 < n_steps)
    def _():
        idx_copy(step + 1, 1 - slot).wait()
        start_rows(1 - slot)

    @pl.when(step + 2 < n_steps)
    def _():
        idx_copy(step + 2, slot).start()

    def drain():
        def body(r, c):
            row_copy(0, slot, r).wait()
            return c
        lax.fori_loop(0, n_rows, body, 0, unroll=8)

    return slot, drain


def _moe_kernel(be_ref, nu_ref, tok_hbm, gate_ref, h_hbm, wg_ref, wu_ref, wd_ref, ys_ref,
                tok_s, buf, isem, rsem):
    del be_ref
    b = pl.program_id(0)
    n_used = nu_ref[0]
    slot, drain = _gather_pipeline(b, n_used, tok_hbm, tok_s, h_hbm, buf, isem, rsem)

    @pl.when(b < n_used)
    def _():
        drain()
        xb = buf[slot].astype(BF16)
        gate = _dot(xb, wg_ref[0])
        hid = gate * jax.nn.sigmoid(gate) * _dot(xb, wu_ref[0])
        ys_ref[...] = _dot(hid.astype(BF16), wd_ref[0]) * gate_ref[...]

    @pl.when(b >= n_used)
    def _():
        ys_ref[...] = jnp.zeros_like(ys_ref)


def _moe(h3, slot_tok, slot_gate, block_expert, n_used, w_gate, w_up, w_down):
    n, d = h3.shape
    blk = MOE_BLOCK_ROWS
    n_blocks = slot_tok.shape[0] // blk
    de = w_gate.shape[2]
    grid_spec = pltpu.PrefetchScalarGridSpec(
        num_scalar_prefetch=2,
        grid=(n_blocks,),
        in_specs=[
            pl.BlockSpec(memory_space=pl.ANY),
            pl.BlockSpec((blk, 1), lambda i, be, nu: (i, 0)),
            pl.BlockSpec(memory_space=pl.ANY),
            pl.BlockSpec((1, d, de), lambda i, be, nu: (be[i], 0, 0)),
            pl.BlockSpec((1, d, de), lambda i, be, nu: (be[i], 0, 0)),
            pl.BlockSpec((1, de, d), lambda i, be, nu: (be[i], 0, 0)),
        ],
        out_specs=pl.BlockSpec((blk, d), lambda i, be, nu: (i, 0)),
        scratch_shapes=[pltpu.SMEM((2, blk), I32), pltpu.VMEM((2, blk, d), F32),
                        pltpu.SemaphoreType.DMA((2,)), pltpu.SemaphoreType.DMA((2,))],
    )
    return pl.pallas_call(
        _moe_kernel,
        grid_spec=grid_spec,
        out_shape=jax.ShapeDtypeStruct((n_blocks * blk, d), F32),
        compiler_params=_params("arbitrary"),
        name="moe",
    )(block_expert, n_used, slot_tok.reshape(n_blocks, blk), slot_gate.reshape(-1, 1), h3,
      w_gate.astype(BF16), w_up.astype(BF16), w_down.astype(BF16))


def _route_tables(eid, gates, blk):
    n = eid.shape[0]
    n_assign = n * TOP_K
    flat_e = eid.reshape(-1)
    onehot = (flat_e[:, None] == jnp.arange(N_EXPERTS, dtype=I32)[None, :]).astype(I32)
    csum = jnp.cumsum(onehot, axis=0)
    rank = jnp.sum(csum * onehot, axis=1) - 1
    counts = csum[-1]
    padded = (counts + blk - 1) // blk * blk
    pad_end = jnp.cumsum(padded)
    dest = (pad_end - padded)[flat_e] + rank
    n_blocks = -(-n_assign // blk) + N_EXPERTS
    tok = jnp.arange(n_assign, dtype=I32) // TOP_K
    slot_tok = jnp.zeros((n_blocks * blk,), I32).at[dest].set(tok, unique_indices=True)
    slot_gate = jnp.zeros((n_blocks * blk,), F32).at[dest].set(gates.reshape(-1), unique_indices=True)
    block_start = jnp.arange(n_blocks, dtype=I32) * blk
    block_expert = jnp.minimum(jnp.searchsorted(pad_end, block_start, side='right'), N_EXPERTS - 1).astype(I32)
    n_used = (pad_end[-1:] // blk).astype(I32)
    return slot_tok, slot_gate, block_expert, n_used, dest.reshape(n, TOP_K).astype(I32)


def _combine_kernel(pos_hbm, x2_ref, nw_ref, ys_hbm, out_ref, pos_s, buf, isem, rsem):
    tm = x2_ref.shape[0]
    slot, drain = _gather_pipeline(pl.program_id(0), pl.num_programs(0), pos_hbm, pos_s, ys_hbm,
                                   buf, isem, rsem)
    drain()
    out_ref[...] = _rms(x2_ref[...] + buf[slot, :tm] + buf[slot, tm:], nw_ref[...])


def _combine(x2, ys, pos, norm_w, tm):
    n, d = x2.shape
    nt = n // tm
    pos_t = jnp.transpose(pos.reshape(nt, tm, TOP_K), (0, 2, 1)).reshape(nt, TOP_K * tm)
    return pl.pallas_call(
        _combine_kernel,
        grid=(nt,),
        in_specs=[
            pl.BlockSpec(memory_space=pl.ANY),
            pl.BlockSpec((tm, d), lambda i: (i, 0)),
            pl.BlockSpec((1, d), lambda i: (0, 0)),
            pl.BlockSpec(memory_space=pl.ANY),
        ],
        out_specs=pl.BlockSpec((tm, d), lambda i: (i, 0)),
        out_shape=jax.ShapeDtypeStruct((n, d), F32),
        scratch_shapes=[pltpu.SMEM((2, TOP_K * tm), I32), pltpu.VMEM((2, TOP_K * tm, d), F32),
                        pltpu.SemaphoreType.DMA((2,)), pltpu.SemaphoreType.DMA((2,))],
        compiler_params=_params("arbitrary"),
        name="combine",
    )(pos_t, x2, norm_w.reshape(1, d), ys)


def _tile(s, want):
    return min(s, want)


def kernel(x, mem, norm_mix_w, w_in, mu_shift, w0, w_decay_up, a0, w_iclr_up, w_gate_up, k_k, k_a, r_k, ln_x_w, ln_x_b, s5_lam_re, s5_lam_im, s5_log_dt, s5_b_re, s5_b_im, s5_c_re, s5_c_im, s5_d, w_glu, b_glu, w_mix_out, norm_xattn_w, norm_mem_w, w_q, w_k, w_v, w_o, norm_moe_w, w_router_group, b_router_group, w_router_expert, b_router_expert, w_exp_gate, w_exp_up, w_exp_down, norm_final_w):
    b, s, d = x.shape
    assert norm_mix_w.shape[0] == 1, "one layer"
    p, u = _inproj(x, norm_mix_w[0], w_in[0], mu_shift[0], _tile(s, 512))
    y_rwkv = _rwkv(p, w0[0], w_decay_up[0], a0[0], w_iclr_up[0], w_gate_up[0], k_k[0], k_a[0],
                   r_k[0], ln_x_w[0], ln_x_b[0], _tile(s, 256))
    y_s5 = _s5(u, _s5_tables(s5_lam_re[0], s5_lam_im[0], s5_log_dt[0], s5_b_re[0], s5_b_im[0],
                             s5_c_re[0], s5_c_im[0], s5_d[0]))
    kmem, vmem = _memkv(mem, norm_mem_w[0], w_k[0], w_v[0])
    x2, h3, eid, gates = _post(x, y_rwkv, y_s5, kmem, vmem, w_glu[0], b_glu[0], w_mix_out[0],
                               norm_xattn_w[0], w_q[0], w_o[0], norm_moe_w[0], w_router_group[0],
                               b_router_group[0], w_router_expert[0], b_router_expert[0],
                               _tile(s, 256))
    n = b * s
    eid = eid.reshape(n, ROUTER_LANES)[:, :TOP_K]
    gates = gates.reshape(n, ROUTER_LANES)[:, :TOP_K]
    slot_tok, slot_gate, block_expert, n_used, pos = _route_tables(eid, gates, MOE_BLOCK_ROWS)
    ys = _moe(h3.reshape(n, d), slot_tok, slot_gate, block_expert, n_used,
              w_exp_gate[0], w_exp_up[0], w_exp_down[0])
    out = _combine(x2.reshape(n, d), ys, pos, norm_final_w, _tile(n, 256))
    return out.reshape(b, s, d)
```

```python
import math

import jax
import jax.numpy as jnp
from jax import lax
from jax.experimental import pallas as pl
from jax.experimental.pallas import tpu as pltpu

F32 = jnp.float32
BF16 = jnp.bfloat16
I32 = jnp.int32

HEAD_DIM = 64
D_RWKV = 512
DECAY_RANK = 64
ICLR_RANK = 64
GATE_RANK = 128
D_SHIFT = 3 * D_RWKV + DECAY_RANK + ICLR_RANK + GATE_RANK
S5_CH = 16
S5_STATE = 64
D_S5 = 512
S5_GROUPS = D_S5 // S5_CH
XATTN_HEADS = 4
N_EXPERT_GROUPS = 4
EXPERTS_PER_GROUP = 8
N_EXPERTS = N_EXPERT_GROUPS * EXPERTS_PER_GROUP
TOP_K = 2
RMS_EPS = 1e-6
GN_EPS = 64e-5
L2_EPS = 1e-12

LANES = 128
SUBLANES = 8
VMEM_LIMIT_BYTES = 56 * 1024 * 1024

RWKV_CHUNK = 64
RWKV_CHUNKS_PER_ITER = 2
PAIR = 2 * HEAD_DIM
S5_CHUNK = 16
S5_OCT = LANES // S5_CH
S5_N_OCT = S5_GROUPS // S5_OCT
S5_ROW_W = S5_CHUNK * LANES
S5_STATE_W = S5_OCT * S5_STATE
MOE_BLOCK_ROWS = 256
ROUTER_LANES = 128


def _dot(a, b):
    return jnp.dot(a, b, preferred_element_type=F32)


def _dot_nt(a, b):
    return lax.dot_general(a, b, (((1,), (1,)), ((), ())), preferred_element_type=F32)


def _dot_tn(a, b):
    return lax.dot_general(a, b, (((0,), (0,)), ((), ())), preferred_element_type=F32)


def _split2(x):
    hi = x.astype(BF16)
    lo = (x - hi.astype(F32)).astype(BF16)
    return hi, lo


def _split3(x):
    hi = x.astype(BF16)
    r = x - hi.astype(F32)
    mid = r.astype(BF16)
    lo = (r - mid.astype(F32)).astype(BF16)
    return hi, mid, lo


def _stack3(w):
    hi, lo = _split2(w.astype(F32))
    return jnp.concatenate([hi, hi, lo], axis=0)


def _dot3(x, w3):
    hi, lo = _split2(x)
    return _dot(jnp.concatenate([hi, lo, hi], axis=1), w3)


def _rms(x, w):
    return x * lax.rsqrt(jnp.mean(x * x, axis=-1, keepdims=True) + RMS_EPS) * w


def _softplus(x):
    return jnp.maximum(x, 0.0) + jnp.log1p(jnp.exp(-jnp.abs(x)))


def _params(*sem):
    return pltpu.CompilerParams(dimension_semantics=sem, vmem_limit_bytes=VMEM_LIMIT_BYTES)


def _inproj_kernel(x_ref, nw_ref, w_ref, mu_ref, p_ref, u_ref, carry_ref):
    tm = x_ref.shape[1]

    @pl.when(pl.program_id(1) == 0)
    def _():
        carry_ref[...] = jnp.zeros_like(carry_ref)

    h = _rms(x_ref[0], nw_ref[...]).astype(BF16)
    proj = _dot(h, w_ref[...])
    z = proj[:, :D_SHIFT]
    row = lax.broadcasted_iota(I32, z.shape, 0)
    prev = jnp.where(row == 0, carry_ref[0:1, :], pltpu.roll(z, 1, 0))
    carry_ref[0:1, :] = z[tm - 1:tm, :]
    p_ref[0] = z + (prev - z) * mu_ref[...]
    for o in range(S5_N_OCT):
        u = proj[:, D_SHIFT + o * LANES:D_SHIFT + (o + 1) * LANES].astype(BF16)
        u_ref[o] = u.reshape(tm // S5_CHUNK, S5_CHUNK, LANES)


def _inproj(x, norm_w, w_in, mu, tm):
    b, s, d = x.shape
    d_in = w_in.shape[1]
    assert d_in - D_SHIFT == D_S5
    return pl.pallas_call(
        _inproj_kernel,
        grid=(b, s // tm),
        in_specs=[
            pl.BlockSpec((1, tm, d), lambda i, j: (i, j, 0)),
            pl.BlockSpec((1, d), lambda i, j: (0, 0)),
            pl.BlockSpec((d, d_in), lambda i, j: (0, 0)),
            pl.BlockSpec((1, D_SHIFT), lambda i, j: (0, 0)),
        ],
        out_specs=[
            pl.BlockSpec((1, tm, D_SHIFT), lambda i, j: (i, j, 0)),
            pl.BlockSpec((S5_N_OCT, tm // S5_CHUNK, None, S5_CHUNK, LANES), lambda i, j: (0, j, i, 0, 0)),
        ],
        out_shape=[
            jax.ShapeDtypeStruct((b, s, D_SHIFT), F32),
            jax.ShapeDtypeStruct((S5_N_OCT, s // S5_CHUNK, b, S5_CHUNK, LANES), BF16),
        ],
        scratch_shapes=[pltpu.VMEM((SUBLANES, D_SHIFT), F32)],
        compiler_params=_params("arbitrary", "arbitrary"),
        name="inproj",
    )(x, norm_w.reshape(1, d), w_in.astype(BF16), mu.reshape(1, D_SHIFT))


def _headsum(x, ones2):
    hi, lo = _split2(x)
    return _dot(jnp.concatenate([hi, lo], axis=1), ones2)


def _rwkv_kernel(p_ref, wcat_ref, wg_ref, vec_ref, tri_ref, ones2_ref, y_ref,
                 lw_s, cum_s, a_s, g_s, h_s):
    t2 = p_ref.shape[1]
    c = RWKV_CHUNK
    n_pairs = D_RWKV // PAIR

    @pl.when(pl.program_id(1) == 0)
    def _():
        h_s[...] = jnp.zeros_like(h_s)

    lane_t = lax.broadcasted_iota(I32, (t2, LANES), 1)
    lo_col = 3 * D_RWKV
    x = p_ref[0, :, lo_col:lo_col + LANES]
    zz = _dot3(jnp.where(lane_t < DECAY_RANK, jnp.tanh(x), x), wcat_ref[...])
    lw = -jnp.exp(-_softplus(-(vec_ref[0:1, :] + zz[:, :D_RWKV])) - 0.5)
    lw_s[...] = lw
    hi, mid, lo = _split3(lw)
    c3 = _dot(tri_ref[...], jnp.concatenate([hi, mid, lo], axis=1))
    cum_s[...] = c3[:, :D_RWKV] + c3[:, D_RWKV:2 * D_RWKV] + c3[:, 2 * D_RWKV:]
    a_s[...] = jax.nn.sigmoid(vec_ref[1:2, :] + zz[:, D_RWKV:])
    dg = p_ref[0, :, lo_col + LANES:lo_col + 2 * LANES]
    g_s[...] = _dot(jax.nn.sigmoid(dg).astype(BF16), wg_ref[...])

    row = lax.broadcasted_iota(I32, (PAIR, PAIR), 0)
    col = lax.broadcasted_iota(I32, (PAIR, PAIR), 1)
    same_head = (row >= c) == (col >= c)
    mask_strict = same_head & (col < row)
    mask_incl = same_head & (col <= row)
    eye = row == col
    head0 = lax.broadcasted_iota(I32, (c, PAIR), 1) < HEAD_DIM
    ones2 = ones2_ref[...]

    def stack(v):
        return jnp.concatenate([jnp.where(head0, v, 0.0), jnp.where(head0, 0.0, v)], axis=0)

    def cat0(a, b):
        return jnp.concatenate([a, b], axis=0)

    def body(it, carry):
        streams = [(ci, j) for ci in range(RWKV_CHUNKS_PER_ITER) for j in range(n_pairs)]
        ns = range(len(streams))
        rows_of = [pl.ds(pl.multiple_of((it * RWKV_CHUNKS_PER_ITER + ci) * c, c), c)
                   for ci in range(RWKV_CHUNKS_PER_ITER)]

        def ld(ref, s, off=0):
            ci, j = streams[s]
            return ref[rows_of[ci], off + j * PAIR:off + (j + 1) * PAIR]

        def ldp(s, off):
            ci, j = streams[s]
            return p_ref[0, rows_of[ci], off + j * PAIR:off + (j + 1) * PAIR]

        def vec(i, s):
            j = streams[s][1]
            return vec_ref[i:i + 1, j * PAIR:(j + 1) * PAIR]

        r = [ldp(s, 0) for s in ns]
        k = [ldp(s, D_RWKV) for s in ns]
        v = [ldp(s, 2 * D_RWKV) for s in ns]
        kk = [k[s] * vec(2, s) for s in ns]
        ss = [_headsum(kk[s] * kk[s], ones2) for s in ns]
        a_t, r_t, v_s, km, w_end, lhs, rhs, bc_end, bk_end = [], [], [], [], [], [], [], [], []
        for s in ns:
            lw_c, cum, a = ld(lw_s, s), ld(cum_s, s), ld(a_s, s)
            kkn = kk[s] / jnp.maximum(jnp.sqrt(ss[s]), L2_EPS)
            km_s = k[s] * (1.0 + (a - 1.0) * vec(3, s))
            beta = kkn * a
            cum_end = cum[c - 1:c, :]
            e_neg = jnp.exp(-cum)
            e_end = jnp.exp(cum_end - cum)
            a_bf = stack(-kkn * jnp.exp(cum - lw_c)).astype(BF16)
            r_f = stack(r[s] * jnp.exp(cum))
            b_t = (beta * e_neg).astype(BF16)
            k_t = (km_s * e_neg).astype(BF16)
            bc = stack(beta * e_end).astype(BF16)
            a_t.append(a_bf)
            r_t.append(r_f)
            v_s.append(stack(v[s]).astype(BF16))
            km.append(km_s)
            w_end.append(jnp.exp(cum_end))
            lhs.append(cat0(a_bf, r_f.astype(BF16)))
            rhs.append(jnp.concatenate([b_t, b_t, k_t, k_t], axis=0))
            bc_end.append(bc)
            bk_end.append(cat0(bc, stack(km_s * e_end).astype(BF16)))

        gram = [_dot_nt(lhs[s], rhs[s]) for s in ns]
        l_ab = [jnp.where(mask_strict, gram[s][:PAIR, :PAIR], 0.0) for s in ns]
        l_ak = [jnp.where(mask_strict, gram[s][:PAIR, PAIR:], 0.0).astype(BF16) for s in ns]
        m_cat = [jnp.concatenate([jnp.where(mask_incl, gram[s][PAIR:, :PAIR], 0.0),
                                  jnp.where(mask_incl, gram[s][PAIR:, PAIR:], 0.0)],
                                 axis=1).astype(BF16) for s in ns]

        inv = [jnp.where(eye, 1.0, l_ab[s]) for s in ns]
        lp = [l_ab[s].astype(BF16) for s in ns]
        n = 1
        while 2 * n < c:
            lp = [_dot(lp[s], lp[s]).astype(BF16) for s in ns]
            inv = [inv[s] + _dot(inv[s].astype(BF16), lp[s]) for s in ns]
            n *= 2
        inv = [inv[s].astype(BF16) for s in ns]

        a_p = [_dot(inv[s], a_t[s]).astype(BF16) for s in ns]
        w1 = [_dot(l_ak[s], v_s[s]).astype(BF16) for s in ns]
        uv = [cat0(_dot(inv[s], w1[s]).astype(BF16), v_s[s]) for s in ns]
        r_p = [(r_t[s] + _dot(m_cat[s][:, :PAIR], a_p[s])).astype(BF16) for s in ns]
        y_p = [_dot(m_cat[s], uv[s]) for s in ns]
        g_t = [(jnp.where(eye, w_end[s], 0.0) + _dot_tn(a_p[s], bc_end[s])).astype(BF16) for s in ns]
        h_p = [_dot_tn(uv[s], bk_end[s]) for s in ns]

        state = [h_s[j] for j in range(n_pairs)]
        y = [None] * len(streams)
        for ci in range(RWKV_CHUNKS_PER_ITER):
            base = ci * n_pairs
            sb = [state[j].astype(BF16) for j in range(n_pairs)]
            for j in range(n_pairs):
                ys = _dot_nt(r_p[base + j], sb[j]) + y_p[base + j]
                y[base + j] = ys[:c] + ys[c:]
            state = [_dot(sb[j], g_t[base + j]) + h_p[base + j] for j in range(n_pairs)]
        for j in range(n_pairs):
            h_s[j] = state[j]

        mu = [_headsum(y[s], ones2) * (1.0 / HEAD_DIM) for s in ns]
        d = [y[s] - mu[s] for s in ns]
        var = [_headsum(d[s] * d[s], ones2) * (1.0 / HEAD_DIM) for s in ns]
        bonus = [_headsum(r[s] * km[s] * vec(4, s), ones2) for s in ns]
        for s in ns:
            ci, j = streams[s]
            yn = d[s] * lax.rsqrt(var[s] + GN_EPS) * vec(5, s) + vec(6, s)
            y_ref[0, rows_of[ci], j * PAIR:(j + 1) * PAIR] = (
                (yn + bonus[s] * v[s]) * ld(g_s, s)).astype(BF16)
        return carry

    lax.fori_loop(0, t2 // (c * RWKV_CHUNKS_PER_ITER), body, 0)


def _rwkv(p, w0, w_decay_up, a0, w_iclr_up, w_gate_up, k_k, k_a, r_k, ln_w, ln_b, t2):
    b, s, _ = p.shape
    zeros = jnp.zeros((DECAY_RANK, D_RWKV), F32)
    wcat = jnp.concatenate([jnp.concatenate([w_decay_up, zeros], axis=1),
                            jnp.concatenate([zeros, w_iclr_up], axis=1)], axis=0)
    vec = jnp.stack([w0, a0, k_k, k_a, r_k.reshape(-1), ln_w, ln_b, jnp.zeros_like(w0)])
    ti = jnp.arange(t2)
    tri = ((ti[:, None] // RWKV_CHUNK == ti[None, :] // RWKV_CHUNK) & (ti[None, :] <= ti[:, None])).astype(BF16)
    li = jnp.arange(PAIR)
    ones_bd = (li[:, None] // HEAD_DIM == li[None, :] // HEAD_DIM).astype(BF16)
    ones2 = jnp.concatenate([ones_bd, ones_bd], axis=0)
    const = lambda shape: pl.BlockSpec(shape, lambda i, j: (0,) * len(shape))
    return pl.pallas_call(
        _rwkv_kernel,
        grid=(b, s // t2),
        in_specs=[
            pl.BlockSpec((1, t2, D_SHIFT), lambda i, j: (i, j, 0)),
            const((3 * LANES, 2 * D_RWKV)),
            const((GATE_RANK, D_RWKV)),
            const((SUBLANES, D_RWKV)),
            const((t2, t2)),
            const((2 * PAIR, PAIR)),
        ],
        out_specs=pl.BlockSpec((1, t2, D_RWKV), lambda i, j: (i, j, 0)),
        out_shape=jax.ShapeDtypeStruct((b, s, D_RWKV), BF16),
        scratch_shapes=[pltpu.VMEM((t2, D_RWKV), F32)] * 4
        + [pltpu.VMEM((D_RWKV // PAIR, PAIR, PAIR), F32)],
        compiler_params=_params("arbitrary", "arbitrary"),
        name="rwkv",
    )(p, _stack3(wcat), w_gate_up.astype(BF16), vec, tri, ones2)


def _s5_kernel(u_ref, m_ref, pre_ref, pim_ref, qre_ref, qim_ref, lam_ref, y_ref, xr_s, xi_s, cr_s, ci_s):
    nb = SUBLANES

    @pl.when(pl.program_id(1) == 0)
    def _():
        cr_s[...] = jnp.zeros_like(cr_s)
        ci_s[...] = jnp.zeros_like(ci_s)

    u = u_ref[0]
    xr_s[...] = _dot(u, pre_ref[0])
    xi_s[...] = _dot(u, pim_ref[0])
    lr = jnp.broadcast_to(lam_ref[0, 0:1, :], (nb, S5_STATE_W))
    li = jnp.broadcast_to(lam_ref[0, 1:2, :], (nb, S5_STATE_W))

    def step(ci, carry):
        xr, xi = carry
        rows = pl.ds(pl.multiple_of(ci * nb, nb), nb)
        ar = xr_s[rows, :]
        ai = xi_s[rows, :]
        xr_s[rows, :] = xr
        xi_s[rows, :] = xi
        return lr * xr - li * xi + ar, lr * xi + li * xr + ai

    xr, xi = lax.fori_loop(0, u.shape[0] // nb, step, (cr_s[...], ci_s[...]), unroll=8)
    cr_s[...] = xr
    ci_s[...] = xi
    y_ref[0] = (_dot(u, m_ref[0]) + _dot(xr_s[...].astype(BF16), qre_ref[0])
                + _dot(xi_s[...].astype(BF16), qim_ref[0]))


def _s5_tables(lam_re, lam_im, log_dt, b_re, b_im, c_re, c_im, d_skip):
    tc = S5_CHUNK
    dt = jnp.exp(log_dt)[:, None]
    ar, ai = lam_re * dt, lam_im * dt
    taus = jnp.arange(tc + 1, dtype=F32)[:, None, None]
    mag = jnp.exp(ar[None] * taus)
    pw_re, pw_im = mag * jnp.cos(ai[None] * taus), mag * jnp.sin(ai[None] * taus)
    x_re, x_im = pw_re[1] - 1.0, pw_im[1]
    den = lam_re * lam_re + lam_im * lam_im
    co_re = (x_re * lam_re + x_im * lam_im) / den
    co_im = (x_im * lam_re - x_re * lam_im) / den
    bb_re = co_re[..., None] * b_re - co_im[..., None] * b_im
    bb_im = co_re[..., None] * b_im + co_im[..., None] * b_re
    w_re = pw_re[:tc, :, :, None] * bb_re[None] - pw_im[:tc, :, :, None] * bb_im[None]
    w_im = pw_re[:tc, :, :, None] * bb_im[None] + pw_im[:tc, :, :, None] * bb_re[None]
    kern = jnp.einsum('gjn,tgni->tgji', c_re, w_re) - jnp.einsum('gjn,tgni->tgji', c_im, w_im)
    kern = kern.at[0].add(d_skip[:, :, None] * jnp.eye(S5_CH, dtype=F32))
    lag = jnp.arange(tc)[None, :] - jnp.arange(tc)[:, None]
    kst = jnp.where((lag >= 0)[:, :, None, None, None], kern[jnp.clip(lag, 0)], 0.0)
    eye = jnp.eye(S5_OCT, dtype=F32)
    k6 = kst.reshape(tc, tc, S5_N_OCT, S5_OCT, S5_CH, S5_CH)
    m8 = jnp.einsum('stogji,gh->osgithj', k6, eye).reshape(S5_N_OCT, S5_ROW_W, S5_ROW_W)
    flip = tc - 1 - jnp.arange(tc)

    def p_table(w):
        p6 = w[flip].reshape(tc, S5_N_OCT, S5_OCT, S5_STATE, S5_CH)
        return jnp.einsum('sogni,gh->osgihn', p6, eye).reshape(S5_N_OCT, S5_ROW_W, S5_STATE_W)

    q_re = c_re[None] * pw_re[1:, :, None, :] - c_im[None] * pw_im[1:, :, None, :]
    q_im = c_re[None] * pw_im[1:, :, None, :] + c_im[None] * pw_re[1:, :, None, :]

    def q_table(q):
        q6 = q.reshape(tc, S5_N_OCT, S5_OCT, S5_CH, S5_STATE)
        return jnp.einsum('togjn,gh->ognthj', q6, eye).reshape(S5_N_OCT, S5_STATE_W, S5_ROW_W)

    lam2 = jnp.stack([pw_re[tc].reshape(S5_N_OCT, S5_STATE_W), pw_im[tc].reshape(S5_N_OCT, S5_STATE_W)], axis=1)
    return (m8.astype(BF16), p_table(w_re).astype(BF16), p_table(w_im).astype(BF16),
            q_table(q_re).astype(BF16), q_table(-q_im).astype(BF16), lam2)


def _s5(u, tables, rt):
    n_oct, nc, b, _, _ = u.shape
    assert b == SUBLANES, "one block step of the state scan handles one sublane tile of batch rows"
    rows = nc * b
    m8, pre, pim, qre, qim, lam2 = tables
    per = lambda a, c_: pl.BlockSpec((1, a, c_), lambda o, i: (o, 0, 0))
    tile = pl.BlockSpec((1, rt, S5_ROW_W), lambda o, i: (o, i, 0))
    y = pl.pallas_call(
        _s5_kernel,
        grid=(n_oct, rows // rt),
        in_specs=[tile, per(S5_ROW_W, S5_ROW_W), per(S5_ROW_W, S5_STATE_W), per(S5_ROW_W, S5_STATE_W),
                  per(S5_STATE_W, S5_ROW_W), per(S5_STATE_W, S5_ROW_W), per(2, S5_STATE_W)],
        out_specs=tile,
        out_shape=jax.ShapeDtypeStruct((n_oct, rows, S5_ROW_W), F32),
        scratch_shapes=[pltpu.VMEM((rt, S5_STATE_W), F32)] * 2 + [pltpu.VMEM((SUBLANES, S5_STATE_W), F32)] * 2,
        compiler_params=_params("arbitrary", "arbitrary"),
        name="s5",
    )(u.reshape(n_oct, rows, S5_ROW_W), m8, pre, pim, qre, qim, lam2)
    return y.reshape(n_oct, nc, b, S5_CHUNK, LANES)


def _memkv_kernel(mem_ref, nw_ref, wk_ref, wv_ref, k_ref, v_ref):
    m = _rms(mem_ref[0], nw_ref[...]).astype(BF16)
    k_ref[0] = _dot(m, wk_ref[...]).astype(BF16)
    v_ref[0] = _dot(m, wv_ref[...]).astype(BF16)


def _memkv(mem, norm_w, w_k, w_v):
    b, nm, d = mem.shape
    const = lambda shape: pl.BlockSpec(shape, lambda i: (0,) * len(shape))
    blk = pl.BlockSpec((1, nm, d), lambda i: (i, 0, 0))
    return pl.pallas_call(
        _memkv_kernel,
        grid=(b,),
        in_specs=[blk, const((1, d)), const((d, d)), const((d, d))],
        out_specs=[blk, blk],
        out_shape=[jax.ShapeDtypeStruct((b, nm, d), BF16)] * 2,
        compiler_params=_params("arbitrary"),
        name="memkv",
    )(mem, norm_w.reshape(1, d), w_k.astype(BF16), w_v.astype(BF16))


def _post_kernel(x_ref, yr_ref, y5_ref, km_ref, vm_ref, wglu_ref, bglu_ref, wmo_ref, nx_ref,
                 wq_ref, wo_ref, nm_ref, wr_ref, br_ref, x2_ref, h3_ref, eid_ref, gate_ref):
    tm, d = x_ref.shape[1], x_ref.shape[2]
    hd = d // XATTN_HEADS
    y = jnp.concatenate([y5_ref[o].reshape(tm, LANES) for o in range(S5_N_OCT)], axis=1)
    z = 0.5 * y * (1.0 + jnp.tanh(math.sqrt(2.0 / math.pi) * (y + 0.044715 * (y * y * y))))
    glu = jax.nn.sigmoid(_dot(z.astype(BF16), wglu_ref[...]) + bglu_ref[...])
    x1 = (x_ref[0] + _dot(yr_ref[0], wmo_ref[:D_RWKV, :])
          + _dot((z * glu).astype(BF16), wmo_ref[D_RWKV:, :]))

    h = _rms(x1, nx_ref[...]).astype(BF16)
    q = (_dot(h, wq_ref[...]) * (hd ** -0.5)).astype(BF16)
    heads = []
    for i in range(XATTN_HEADS):
        sl = slice(i * hd, (i + 1) * hd)
        sc = _dot_nt(q[:, sl], km_ref[0, :, sl])
        e = jnp.exp(sc - jnp.max(sc, axis=-1, keepdims=True))
        o = _dot(e.astype(BF16), vm_ref[0, :, sl]) / jnp.sum(e, axis=-1, keepdims=True)
        heads.append(o.astype(BF16))
    x2 = x1 + _dot(jnp.concatenate(heads, axis=1), wo_ref[...])
    x2_ref[0] = x2
    h3 = _rms(x2, nm_ref[...])
    h3_ref[0] = h3

    lg = _dot3(h3, wr_ref[...]) + br_ref[...]
    lane = lax.broadcasted_iota(I32, lg.shape, 1)
    lanef = lane.astype(F32)
    neg = -jnp.inf
    no_lane = float(ROUTER_LANES)
    l1 = jnp.where(lane < N_EXPERT_GROUPS, lg, neg)
    m1 = jnp.max(l1, axis=-1, keepdims=True)
    grp = jnp.min(jnp.where(l1 == m1, lanef, no_lane), axis=-1, keepdims=True)
    g1 = 1.0 / jnp.sum(jnp.exp(l1 - m1), axis=-1, keepdims=True)
    first = N_EXPERT_GROUPS + EXPERTS_PER_GROUP * grp
    l2 = jnp.where((lanef >= first) & (lanef < first + EXPERTS_PER_GROUP), lg, neg)
    v1 = jnp.max(l2, axis=-1, keepdims=True)
    i1 = jnp.min(jnp.where(l2 == v1, lanef, no_lane), axis=-1, keepdims=True)
    l2 = jnp.where(lanef == i1, neg, l2)
    v2 = jnp.max(l2, axis=-1, keepdims=True)
    i2 = jnp.min(jnp.where(l2 == v2, lanef, no_lane), axis=-1, keepdims=True)
    e2 = jnp.exp(v2 - v1)
    den = 1.0 + e2
    eid_ref[0] = jnp.where(lane == 0, i1 - N_EXPERT_GROUPS,
                           jnp.where(lane == 1, i2 - N_EXPERT_GROUPS, 0.0)).astype(I32)
    gate_ref[0] = jnp.where(lane == 0, g1 / den, jnp.where(lane == 1, g1 * e2 / den, 0.0))


def _post(x, y_rwkv, y_s5, kmem, vmem, w_glu, b_glu, w_mix_out, norm_x, w_q, w_o, norm_moe,
          w_rg, b_rg, w_re, b_re, tm):
    b, s, d = x.shape
    nm = kmem.shape[1]
    n_logits = N_EXPERT_GROUPS + N_EXPERTS
    w_router = jnp.concatenate(
        [w_rg, jnp.transpose(w_re, (1, 0, 2)).reshape(d, N_EXPERTS),
         jnp.zeros((d, ROUTER_LANES - n_logits), F32)], axis=1)
    b_router = jnp.concatenate([b_rg, b_re.reshape(-1), jnp.zeros((ROUTER_LANES - n_logits,), F32)])
    const = lambda shape: pl.BlockSpec(shape, lambda i, j: (0,) * len(shape))
    tok = lambda w: pl.BlockSpec((1, tm, w), lambda i, j: (i, j, 0))
    mem = pl.BlockSpec((1, nm, d), lambda i, j: (i, 0, 0))
    s5_tile = pl.BlockSpec((S5_N_OCT, tm // S5_CHUNK, None, S5_CHUNK, LANES), lambda i, j: (0, j, i, 0, 0))
    return pl.pallas_call(
        _post_kernel,
        grid=(b, s // tm),
        in_specs=[tok(d), tok(D_RWKV), s5_tile, mem, mem,
                  const((D_S5, D_S5)), const((1, D_S5)), const((D_RWKV + D_S5, d)), const((1, d)),
                  const((d, d)), const((d, d)), const((1, d)),
                  const((3 * d, ROUTER_LANES)), const((1, ROUTER_LANES))],
        out_specs=[tok(d), tok(d), tok(ROUTER_LANES), tok(ROUTER_LANES)],
        out_shape=[jax.ShapeDtypeStruct((b, s, d), F32), jax.ShapeDtypeStruct((b, s, d), F32),
                   jax.ShapeDtypeStruct((b, s, ROUTER_LANES), I32),
                   jax.ShapeDtypeStruct((b, s, ROUTER_LANES), F32)],
        compiler_params=_params("arbitrary", "arbitrary"),
        name="post",
    )(x, y_rwkv, y_s5, kmem, vmem, w_glu.astype(BF16), b_glu.reshape(1, D_S5),
      w_mix_out.astype(BF16), norm_x.reshape(1, d), w_q.astype(BF16), w_o.astype(BF16),
      norm_moe.reshape(1, d), _stack3(w_router), b_router.reshape(1, ROUTER_LANES))


def _gather_pipeline(step, n_steps, idx_hbm, idx_s, src_hbm, buf, isem, rsem):
    n_rows = buf.shape[1]
    slot = step % 2

    def idx_copy(t, s):
        return pltpu.make_async_copy(idx_hbm.at[t], idx_s.at[s], isem.at[s])

    def row_copy(src_row, s, r):
        return pltpu.make_async_copy(src_hbm.at[pl.ds(src_row, 1)], buf.at[s, pl.ds(r, 1)], rsem.at[s])

    def start_rows(s):
        def body(r, c):
            row_copy(idx_s[s, r], s, r).start()
            return c
        lax.fori_loop(0, n_rows, body, 0, unroll=8)

    @pl.when(step == 0)
    def _():
        idx_copy(0, 0).start()
        idx_copy(0, 0).wait()
        start_rows(0)

        @pl.when(n_steps > 1)
        def _():
            idx_copy(1, 1).start()

    @pl.when(step + 1 < n_steps)
    def _():
        idx_copy(step + 1, 1 - slot).wait()
        start_rows(1 - slot)

    @pl.when(step + 2 < n_steps)
    def _():
        idx_copy(step + 2, slot).start()

    def drain():
        def body(r, c):
            row_copy(0, slot, r).wait()
            return c
        lax.fori_loop(0, n_rows, body, 0, unroll=8)

    return slot, drain


def _moe_kernel(be_ref, nu_ref, tok_hbm, gate_ref, h_hbm, wg_ref, wu_ref, wd_ref, ys_ref,
                tok_s, buf, isem, rsem):
    del be_ref
    b = pl.program_id(0)
    n_used = nu_ref[0]
    slot, drain = _gather_pipeline(b, n_used, tok_hbm, tok_s, h_hbm, buf, isem, rsem)

    @pl.when(b < n_used)
    def _():
        drain()
        xb = buf[slot].astype(BF16)
        gate = _dot(xb, wg_ref[0])
        hid = gate * jax.nn.sigmoid(gate) * _dot(xb, wu_ref[0])
        ys_ref[...] = _dot(hid.astype(BF16), wd_ref[0]) * gate_ref[...]

    @pl.when(b >= n_used)
    def _():
        ys_ref[...] = jnp.zeros_like(ys_ref)


def _moe(h3, slot_tok, slot_gate, block_expert, n_used, w_gate, w_up, w_down):
    n, d = h3.shape
    blk = MOE_BLOCK_ROWS
    n_blocks = slot_tok.shape[0] // blk
    de = w_gate.shape[2]
    grid_spec = pltpu.PrefetchScalarGridSpec(
        num_scalar_prefetch=2,
        grid=(n_blocks,),
        in_specs=[
            pl.BlockSpec(memory_space=pl.ANY),
            pl.BlockSpec((blk, 1), lambda i, be, nu: (i, 0)),
            pl.BlockSpec(memory_space=pl.ANY),
            pl.BlockSpec((1, d, de), lambda i, be, nu: (be[i], 0, 0)),
            pl.BlockSpec((1, d, de), lambda i, be, nu: (be[i], 0, 0)),
            pl.BlockSpec((1, de, d), lambda i, be, nu: (be[i], 0, 0)),
        ],
        out_specs=pl.BlockSpec((blk, d), lambda i, be, nu: (i, 0)),
        scratch_shapes=[pltpu.SMEM((2, blk), I32), pltpu.VMEM((2, blk, d), F32),
                        pltpu.SemaphoreType.DMA((2,)), pltpu.SemaphoreType.DMA((2,))],
    )
    return pl.pallas_call(
        _moe_kernel,
        grid_spec=grid_spec,
        out_shape=jax.ShapeDtypeStruct((n_blocks * blk, d), F32),
        compiler_params=_params("arbitrary"),
        name="moe",
    )(block_expert, n_used, slot_tok.reshape(n_blocks, blk), slot_gate.reshape(-1, 1), h3,
      w_gate.astype(BF16), w_up.astype(BF16), w_down.astype(BF16))


def _route_tables(eid, gates, blk):
    n = eid.shape[0]
    n_assign = n * TOP_K
    flat_e = eid.reshape(-1)
    onehot = (flat_e[:, None] == jnp.arange(N_EXPERTS, dtype=I32)[None, :]).astype(I32)
    csum = jnp.cumsum(onehot, axis=0)
    rank = jnp.sum(csum * onehot, axis=1) - 1
    counts = csum[-1]
    padded = (counts + blk - 1) // blk * blk
    pad_end = jnp.cumsum(padded)
    dest = (pad_end - padded)[flat_e] + rank
    n_blocks = -(-n_assign // blk) + N_EXPERTS
    tok = jnp.arange(n_assign, dtype=I32) // TOP_K
    slot_tok = jnp.zeros((n_blocks * blk,), I32).at[dest].set(tok, unique_indices=True)
    slot_gate = jnp.zeros((n_blocks * blk,), F32).at[dest].set(gates.reshape(-1), unique_indices=True)
    block_start = jnp.arange(n_blocks, dtype=I32) * blk
    block_expert = jnp.minimum(jnp.searchsorted(pad_end, block_start, side='right'), N_EXPERTS - 1).astype(I32)
    n_used = (pad_end[-1:] // blk).astype(I32)
    return slot_tok, slot_gate, block_expert, n_used, dest.reshape(n, TOP_K).astype(I32)


def _combine_kernel(pos_hbm, x2_ref, nw_ref, ys_hbm, out_ref, pos_s, buf, isem, rsem):
    tm = x2_ref.shape[0]
    slot, drain = _gather_pipeline(pl.program_id(0), pl.num_programs(0), pos_hbm, pos_s, ys_hbm,
                                   buf, isem, rsem)
    drain()
    out_ref[...] = _rms(x2_ref[...] + buf[slot, :tm] + buf[slot, tm:], nw_ref[...])


def _combine(x2, ys, pos, norm_w, tm):
    n, d = x2.shape
    nt = n // tm
    pos_t = jnp.transpose(pos.reshape(nt, tm, TOP_K), (0, 2, 1)).reshape(nt, TOP_K * tm)
    return pl.pallas_call(
        _combine_kernel,
        grid=(nt,),
        in_specs=[
            pl.BlockSpec(memory_space=pl.ANY),
            pl.BlockSpec((tm, d), lambda i: (i, 0)),
            pl.BlockSpec((1, d), lambda i: (0, 0)),
            pl.BlockSpec(memory_space=pl.ANY),
        ],
        out_specs=pl.BlockSpec((tm, d), lambda i: (i, 0)),
        out_shape=jax.ShapeDtypeStruct((n, d), F32),
        scratch_shapes=[pltpu.SMEM((2, TOP_K * tm), I32), pltpu.VMEM((2, TOP_K * tm, d), F32),
                        pltpu.SemaphoreType.DMA((2,)), pltpu.SemaphoreType.DMA((2,))],
        compiler_params=_params("arbitrary"),
        name="combine",
    )(pos_t, x2, norm_w.reshape(1, d), ys)


def _tile(s, want):
    return min(s, want)


def kernel(x, mem, norm_mix_w, w_in, mu_shift, w0, w_decay_up, a0, w_iclr_up, w_gate_up, k_k, k_a, r_k, ln_x_w, ln_x_b, s5_lam_re, s5_lam_im, s5_log_dt, s5_b_re, s5_b_im, s5_c_re, s5_c_im, s5_d, w_glu, b_glu, w_mix_out, norm_xattn_w, norm_mem_w, w_q, w_k, w_v, w_o, norm_moe_w, w_router_group, b_router_group, w_router_expert, b_router_expert, w_exp_gate, w_exp_up, w_exp_down, norm_final_w):
    b, s, d = x.shape
    assert norm_mix_w.shape[0] == 1, "one layer"
    p, u = _inproj(x, norm_mix_w[0], w_in[0], mu_shift[0], _tile(s, 512))
    y_rwkv = _rwkv(p, w0[0], w_decay_up[0], a0[0], w_iclr_up[0], w_gate_up[0], k_k[0], k_a[0],
                   r_k[0], ln_x_w[0], ln_x_b[0], _tile(s, 256))
    y_s5 = _s5(u, _s5_tables(s5_lam_re[0], s5_lam_im[0], s5_log_dt[0], s5_b_re[0], s5_b_im[0],
                             s5_c_re[0], s5_c_im[0], s5_d[0]), _tile(s // S5_CHUNK * b, 512))
    kmem, vmem = _memkv(mem, norm_mem_w[0], w_k[0], w_v[0])
    x2, h3, eid, gates = _post(x, y_rwkv, y_s5, kmem, vmem, w_glu[0], b_glu[0], w_mix_out[0],
                               norm_xattn_w[0], w_q[0], w_o[0], norm_moe_w[0], w_router_group[0],
                               b_router_group[0], w_router_expert[0], b_router_expert[0],
                               _tile(s, 256))
    n = b * s
    eid = eid.reshape(n, ROUTER_LANES)[:, :TOP_K]
    gates = gates.reshape(n, ROUTER_LANES)[:, :TOP_K]
    slot_tok, slot_gate, block_expert, n_used, pos = _route_tables(eid, gates, MOE_BLOCK_ROWS)
    ys = _moe(h3.reshape(n, d), slot_tok, slot_gate, block_expert, n_used,
              w_exp_gate[0], w_exp_up[0], w_exp_down[0])
    out = _combine(x2.reshape(n, d), ys, pos, norm_final_w, _tile(n, 256))
    return out.reshape(b, s, d)
```

```python
import math

import jax
import jax.numpy as jnp
from jax import lax
from jax.experimental import pallas as pl
from jax.experimental.pallas import tpu as pltpu

F32 = jnp.float32
BF16 = jnp.bfloat16
I32 = jnp.int32

HEAD_DIM = 64
D_RWKV = 512
DECAY_RANK = 64
ICLR_RANK = 64
GATE_RANK = 128
D_SHIFT = 3 * D_RWKV + DECAY_RANK + ICLR_RANK + GATE_RANK
S5_CH = 16
S5_STATE = 64
D_S5 = 512
S5_GROUPS = D_S5 // S5_CH
XATTN_HEADS = 4
N_EXPERT_GROUPS = 4
EXPERTS_PER_GROUP = 8
N_EXPERTS = N_EXPERT_GROUPS * EXPERTS_PER_GROUP
TOP_K = 2
RMS_EPS = 1e-6
GN_EPS = 64e-5
L2_EPS = 1e-12

LANES = 128
SUBLANES = 8
VMEM_LIMIT_BYTES = 56 * 1024 * 1024

RWKV_CHUNK = 64
RWKV_CHUNKS_PER_ITER = 2
PAIR = 2 * HEAD_DIM
S5_CHUNK = 16
S5_OCT = LANES // S5_CH
S5_N_OCT = S5_GROUPS // S5_OCT
S5_ROW_W = S5_CHUNK * LANES
S5_STATE_W = S5_OCT * S5_STATE
MOE_BLOCK_ROWS = 256
ROUTER_LANES = 128


def _dot(a, b):
    return jnp.dot(a, b, preferred_element_type=F32)


def _dot_nt(a, b):
    return lax.dot_general(a, b, (((1,), (1,)), ((), ())), preferred_element_type=F32)


def _dot_tn(a, b):
    return lax.dot_general(a, b, (((0,), (0,)), ((), ())), preferred_element_type=F32)


def _split2(x):
    hi = x.astype(BF16)
    lo = (x - hi.astype(F32)).astype(BF16)
    return hi, lo


def _split3(x):
    hi = x.astype(BF16)
    r = x - hi.astype(F32)
    mid = r.astype(BF16)
    lo = (r - mid.astype(F32)).astype(BF16)
    return hi, mid, lo


def _stack3(w):
    hi, lo = _split2(w.astype(F32))
    return jnp.concatenate([hi, hi, lo], axis=0)


def _dot3(x, w3):
    hi, lo = _split2(x)
    return _dot(jnp.concatenate([hi, lo, hi], axis=1), w3)


def _rms(x, w):
    return x * lax.rsqrt(jnp.mean(x * x, axis=-1, keepdims=True) + RMS_EPS) * w


def _softplus(x):
    return jnp.maximum(x, 0.0) + jnp.log1p(jnp.exp(-jnp.abs(x)))


def _params(*sem):
    return pltpu.CompilerParams(dimension_semantics=sem, vmem_limit_bytes=VMEM_LIMIT_BYTES)


def _token_rows(j, n):
    return pl.ds(j, n, stride=SUBLANES)


def _inproj_kernel(x_ref, nw_ref, w_ref, mu_ref, p_ref, u_ref, carry_ref):
    tm = x_ref.shape[1]

    @pl.when(pl.program_id(1) == 0)
    def _():
        carry_ref[...] = jnp.zeros_like(carry_ref)

    h = _rms(x_ref[0], nw_ref[...]).astype(BF16)
    proj = _dot(h, w_ref[...])
    z = proj[:, :D_SHIFT]
    row = lax.broadcasted_iota(I32, z.shape, 0)
    prev = jnp.where(row == 0, carry_ref[0:1, :], pltpu.roll(z, 1, 0))
    carry_ref[0:1, :] = z[tm - 1:tm, :]
    p_ref[0] = z + (prev - z) * mu_ref[...]
    for o in range(S5_N_OCT):
        u = proj[:, D_SHIFT + o * LANES:D_SHIFT + (o + 1) * LANES].astype(BF16)
        u_ref[o] = u.reshape(tm // S5_CHUNK, S5_CHUNK, LANES)


def _inproj(x, norm_w, w_in, mu, tm):
    b, s, d = x.shape
    d_in = w_in.shape[1]
    assert d_in - D_SHIFT == D_S5
    return pl.pallas_call(
        _inproj_kernel,
        grid=(b, s // tm),
        in_specs=[
            pl.BlockSpec((1, tm, d), lambda i, j: (i, j, 0)),
            pl.BlockSpec((1, d), lambda i, j: (0, 0)),
            pl.BlockSpec((d, d_in), lambda i, j: (0, 0)),
            pl.BlockSpec((1, D_SHIFT), lambda i, j: (0, 0)),
        ],
        out_specs=[
            pl.BlockSpec((1, tm, D_SHIFT), lambda i, j: (i, j, 0)),
            pl.BlockSpec((S5_N_OCT, tm // S5_CHUNK, None, S5_CHUNK, LANES), lambda i, j: (0, j, i, 0, 0)),
        ],
        out_shape=[
            jax.ShapeDtypeStruct((b, s, D_SHIFT), F32),
            jax.ShapeDtypeStruct((S5_N_OCT, s // S5_CHUNK, b, S5_CHUNK, LANES), BF16),
        ],
        scratch_shapes=[pltpu.VMEM((SUBLANES, D_SHIFT), F32)],
        compiler_params=_params("arbitrary", "arbitrary"),
        name="inproj",
    )(x, norm_w.reshape(1, d), w_in.astype(BF16), mu.reshape(1, D_SHIFT))


def _headsum(x, ones2):
    hi, lo = _split2(x)
    return _dot(jnp.concatenate([hi, lo], axis=1), ones2)


def _rwkv_kernel(p_ref, wcat_ref, wg_ref, vec_ref, tri_ref, ones2_ref, y_ref,
                 lw_s, cum_s, a_s, g_s, h_s):
    t2 = p_ref.shape[1]
    c = RWKV_CHUNK
    n_pairs = D_RWKV // PAIR

    @pl.when(pl.program_id(1) == 0)
    def _():
        h_s[...] = jnp.zeros_like(h_s)

    lane_t = lax.broadcasted_iota(I32, (t2, LANES), 1)
    lo_col = 3 * D_RWKV
    x = p_ref[0, :, lo_col:lo_col + LANES]
    zz = _dot3(jnp.where(lane_t < DECAY_RANK, jnp.tanh(x), x), wcat_ref[...])
    lw = -jnp.exp(-_softplus(-(vec_ref[0:1, :] + zz[:, :D_RWKV])) - 0.5)
    lw_s[...] = lw
    hi, mid, lo = _split3(lw)
    c3 = _dot(tri_ref[...], jnp.concatenate([hi, mid, lo], axis=1))
    cum_s[...] = c3[:, :D_RWKV] + c3[:, D_RWKV:2 * D_RWKV] + c3[:, 2 * D_RWKV:]
    a_s[...] = jax.nn.sigmoid(vec_ref[1:2, :] + zz[:, D_RWKV:])
    dg = p_ref[0, :, lo_col + LANES:lo_col + 2 * LANES]
    g_s[...] = _dot(jax.nn.sigmoid(dg).astype(BF16), wg_ref[...])

    t_idx = lax.broadcasted_iota(I32, (c, PAIR), 0)
    lane = lax.broadcasted_iota(I32, (c, PAIR), 1)
    s_idx = lane & (HEAD_DIM - 1)
    mask_strict = s_idx < t_idx
    mask_incl = s_idx <= t_idx
    eye_lane = s_idx == t_idx
    head0 = lane < HEAD_DIM
    row_b = lax.broadcasted_iota(I32, (PAIR, PAIR), 0)
    col_b = lax.broadcasted_iota(I32, (PAIR, PAIR), 1)
    eye_block = row_b == col_b
    ones2 = ones2_ref[...]

    def block(v, dtype=BF16):
        return jnp.concatenate([jnp.where(head0, v, 0.0), jnp.where(head0, 0.0, v)], axis=0).astype(dtype)

    def cat0(a, b):
        return jnp.concatenate([a, b], axis=0)

    def body(it, carry):
        streams = [(ci, j) for ci in range(RWKV_CHUNKS_PER_ITER) for j in range(n_pairs)]
        ns = range(len(streams))
        rows_of = [pl.ds(pl.multiple_of((it * RWKV_CHUNKS_PER_ITER + ci) * c, c), c)
                   for ci in range(RWKV_CHUNKS_PER_ITER)]

        def ld(ref, s):
            ci, j = streams[s]
            return ref[rows_of[ci], j * PAIR:(j + 1) * PAIR]

        def ldp(s, off):
            ci, j = streams[s]
            return p_ref[0, rows_of[ci], off + j * PAIR:off + (j + 1) * PAIR]

        def vec(i, s):
            j = streams[s][1]
            return vec_ref[i:i + 1, j * PAIR:(j + 1) * PAIR]

        r = [ldp(s, 0) for s in ns]
        k = [ldp(s, D_RWKV) for s in ns]
        v = [ldp(s, 2 * D_RWKV) for s in ns]
        kk = [k[s] * vec(2, s) for s in ns]
        ss = [_headsum(kk[s] * kk[s], ones2) for s in ns]
        a_blk, v_blk, bk_end, bc_blk, km, w_end, r_t, lhs, rhs = [], [], [], [], [], [], [], [], []
        for s in ns:
            lw_c, cum, a = ld(lw_s, s), ld(cum_s, s), ld(a_s, s)
            kkn = kk[s] / jnp.maximum(jnp.sqrt(ss[s]), L2_EPS)
            km_s = k[s] * (1.0 + (a - 1.0) * vec(3, s))
            beta = kkn * a
            cum_end = cum[c - 1:c, :]
            e_neg = jnp.exp(-cum)
            e_end = jnp.exp(cum_end - cum)
            a_t = -kkn * jnp.exp(cum - lw_c)
            r_f = r[s] * jnp.exp(cum)
            bc = block(beta * e_end)
            km.append(km_s)
            w_end.append(jnp.exp(cum_end))
            r_t.append(r_f)
            a_blk.append(block(a_t))
            v_blk.append(block(v[s]))
            bc_blk.append(bc)
            bk_end.append(cat0(bc, block(km_s * e_end)))
            lhs.append(cat0(a_t, r_f).astype(BF16))
            rhs.append(cat0(block(beta * e_neg), block(km_s * e_neg)))

        gram = [_dot_nt(lhs[s], rhs[s]) for s in ns]
        l_ab = [jnp.where(mask_strict, gram[s][:c, :PAIR], 0.0) for s in ns]
        l_ak = [jnp.where(mask_strict, gram[s][:c, PAIR:], 0.0).astype(BF16) for s in ns]
        m_cat = [jnp.concatenate([jnp.where(mask_incl, gram[s][c:, :PAIR], 0.0),
                                  jnp.where(mask_incl, gram[s][c:, PAIR:], 0.0)],
                                 axis=1).astype(BF16) for s in ns]

        inv = [jnp.where(eye_lane, 1.0, l_ab[s]) for s in ns]
        lp = l_ab
        lp_blk = [block(lp[s]) for s in ns]
        n = 1
        while 2 * n < c:
            lp = [_dot(lp[s].astype(BF16), lp_blk[s]) for s in ns]
            lp_blk = [block(lp[s]) for s in ns]
            inv = [inv[s] + _dot(inv[s].astype(BF16), lp_blk[s]) for s in ns]
            n *= 2
        inv = [inv[s].astype(BF16) for s in ns]

        a_p = [block(_dot(inv[s], a_blk[s])) for s in ns]
        w1 = [block(_dot(l_ak[s], v_blk[s])) for s in ns]
        uv = [cat0(block(_dot(inv[s], w1[s])), v_blk[s]) for s in ns]
        r_p = [(r_t[s] + _dot(m_cat[s][:, :PAIR], a_p[s])).astype(BF16) for s in ns]
        y_p = [_dot(m_cat[s], uv[s]) for s in ns]
        g_t = [(jnp.where(eye_block, w_end[s], 0.0) + _dot_tn(a_p[s], bc_blk[s])).astype(BF16)
               for s in ns]
        h_p = [_dot_tn(uv[s], bk_end[s]) for s in ns]

        state = [h_s[j] for j in range(n_pairs)]
        y = [None] * len(streams)
        for ci in range(RWKV_CHUNKS_PER_ITER):
            base = ci * n_pairs
            for j in range(n_pairs):
                y[base + j] = _dot_nt(r_p[base + j], state[j].astype(BF16)) + y_p[base + j]
            s_lane = [(state[j][:c] + state[j][c:]).astype(BF16) for j in range(n_pairs)]
            state = [block(_dot(s_lane[j], g_t[base + j]), F32) + h_p[base + j]
                     for j in range(n_pairs)]
        for j in range(n_pairs):
            h_s[j] = state[j]

        mu = [_headsum(y[s], ones2) * (1.0 / HEAD_DIM) for s in ns]
        d = [y[s] - mu[s] for s in ns]
        var = [_headsum(d[s] * d[s], ones2) * (1.0 / HEAD_DIM) for s in ns]
        bonus = [_headsum(r[s] * km[s] * vec(4, s), ones2) for s in ns]
        for s in ns:
            ci, j = streams[s]
            yn = d[s] * lax.rsqrt(var[s] + GN_EPS) * vec(5, s) + vec(6, s)
            y_ref[0, rows_of[ci], j * PAIR:(j + 1) * PAIR] = (
                (yn + bonus[s] * v[s]) * ld(g_s, s)).astype(BF16)
        return carry

    lax.fori_loop(0, t2 // (c * RWKV_CHUNKS_PER_ITER), body, 0)


def _rwkv(p, w0, w_decay_up, a0, w_iclr_up, w_gate_up, k_k, k_a, r_k, ln_w, ln_b, t2):
    b, s, _ = p.shape
    zeros = jnp.zeros((DECAY_RANK, D_RWKV), F32)
    wcat = jnp.concatenate([jnp.concatenate([w_decay_up, zeros], axis=1),
                            jnp.concatenate([zeros, w_iclr_up], axis=1)], axis=0)
    vec = jnp.stack([w0, a0, k_k, k_a, r_k.reshape(-1), ln_w, ln_b, jnp.zeros_like(w0)])
    ti = jnp.arange(t2)
    tri = ((ti[:, None] // RWKV_CHUNK == ti[None, :] // RWKV_CHUNK) & (ti[None, :] <= ti[:, None])).astype(BF16)
    li = jnp.arange(PAIR)
    ones_bd = (li[:, None] // HEAD_DIM == li[None, :] // HEAD_DIM).astype(BF16)
    ones2 = jnp.concatenate([ones_bd, ones_bd], axis=0)
    const = lambda shape: pl.BlockSpec(shape, lambda i, j: (0,) * len(shape))
    return pl.pallas_call(
        _rwkv_kernel,
        grid=(b, s // t2),
        in_specs=[
            pl.BlockSpec((1, t2, D_SHIFT), lambda i, j: (i, j, 0)),
            const((3 * LANES, 2 * D_RWKV)),
            const((GATE_RANK, D_RWKV)),
            const((SUBLANES, D_RWKV)),
            const((t2, t2)),
            const((2 * PAIR, PAIR)),
        ],
        out_specs=pl.BlockSpec((1, t2, D_RWKV), lambda i, j: (i, j, 0)),
        out_shape=jax.ShapeDtypeStruct((b, s, D_RWKV), BF16),
        scratch_shapes=[pltpu.VMEM((t2, D_RWKV), F32)] * 4
        + [pltpu.VMEM((D_RWKV // PAIR, PAIR, PAIR), F32)],
        compiler_params=_params("arbitrary", "arbitrary"),
        name="rwkv",
    )(p, _stack3(wcat), w_gate_up.astype(BF16), vec, tri, ones2)


def _s5_kernel(u_ref, m_ref, pre_ref, pim_ref, qre_ref, qim_ref, lam_ref, y_ref, xr_s, xi_s, cr_s, ci_s):
    nb = SUBLANES

    @pl.when(pl.program_id(1) == 0)
    def _():
        cr_s[...] = jnp.zeros_like(cr_s)
        ci_s[...] = jnp.zeros_like(ci_s)

    u = u_ref[0]
    xr_s[...] = _dot(u, pre_ref[0])
    xi_s[...] = _dot(u, pim_ref[0])
    lr = jnp.broadcast_to(lam_ref[0, 0:1, :], (nb, S5_STATE_W))
    li = jnp.broadcast_to(lam_ref[0, 1:2, :], (nb, S5_STATE_W))

    def step(ci, carry):
        xr, xi = carry
        rows = pl.ds(pl.multiple_of(ci * nb, nb), nb)
        ar = xr_s[rows, :]
        ai = xi_s[rows, :]
        xr_s[rows, :] = xr
        xi_s[rows, :] = xi
        return lr * xr - li * xi + ar, lr * xi + li * xr + ai

    xr, xi = lax.fori_loop(0, u.shape[0] // nb, step, (cr_s[...], ci_s[...]), unroll=8)
    cr_s[...] = xr
    ci_s[...] = xi
    y_ref[0] = (_dot(u, m_ref[0]) + _dot(xr_s[...].astype(BF16), qre_ref[0])
                + _dot(xi_s[...].astype(BF16), qim_ref[0]))


def _s5_tables(lam_re, lam_im, log_dt, b_re, b_im, c_re, c_im, d_skip):
    tc = S5_CHUNK
    dt = jnp.exp(log_dt)[:, None]
    ar, ai = lam_re * dt, lam_im * dt
    taus = jnp.arange(tc + 1, dtype=F32)[:, None, None]
    mag = jnp.exp(ar[None] * taus)
    pw_re, pw_im = mag * jnp.cos(ai[None] * taus), mag * jnp.sin(ai[None] * taus)
    x_re, x_im = pw_re[1] - 1.0, pw_im[1]
    den = lam_re * lam_re + lam_im * lam_im
    co_re = (x_re * lam_re + x_im * lam_im) / den
    co_im = (x_im * lam_re - x_re * lam_im) / den
    bb_re = co_re[..., None] * b_re - co_im[..., None] * b_im
    bb_im = co_re[..., None] * b_im + co_im[..., None] * b_re
    w_re = pw_re[:tc, :, :, None] * bb_re[None] - pw_im[:tc, :, :, None] * bb_im[None]
    w_im = pw_re[:tc, :, :, None] * bb_im[None] + pw_im[:tc, :, :, None] * bb_re[None]
    kern = jnp.einsum('gjn,tgni->tgji', c_re, w_re) - jnp.einsum('gjn,tgni->tgji', c_im, w_im)
    kern = kern.at[0].add(d_skip[:, :, None] * jnp.eye(S5_CH, dtype=F32))
    lag = jnp.arange(tc)[None, :] - jnp.arange(tc)[:, None]
    kst = jnp.where((lag >= 0)[:, :, None, None, None], kern[jnp.clip(lag, 0)], 0.0)
    eye = jnp.eye(S5_OCT, dtype=F32)
    k6 = kst.reshape(tc, tc, S5_N_OCT, S5_OCT, S5_CH, S5_CH)
    m8 = jnp.einsum('stogji,gh->osgithj', k6, eye).reshape(S5_N_OCT, S5_ROW_W, S5_ROW_W)
    flip = tc - 1 - jnp.arange(tc)

    def p_table(w):
        p6 = w[flip].reshape(tc, S5_N_OCT, S5_OCT, S5_STATE, S5_CH)
        return jnp.einsum('sogni,gh->osgihn', p6, eye).reshape(S5_N_OCT, S5_ROW_W, S5_STATE_W)

    q_re = c_re[None] * pw_re[1:, :, None, :] - c_im[None] * pw_im[1:, :, None, :]
    q_im = c_re[None] * pw_im[1:, :, None, :] + c_im[None] * pw_re[1:, :, None, :]

    def q_table(q):
        q6 = q.reshape(tc, S5_N_OCT, S5_OCT, S5_CH, S5_STATE)
        return jnp.einsum('togjn,gh->ognthj', q6, eye).reshape(S5_N_OCT, S5_STATE_W, S5_ROW_W)

    lam2 = jnp.stack([pw_re[tc].reshape(S5_N_OCT, S5_STATE_W), pw_im[tc].reshape(S5_N_OCT, S5_STATE_W)], axis=1)
    return (m8.astype(BF16), p_table(w_re).astype(BF16), p_table(w_im).astype(BF16),
            q_table(q_re).astype(BF16), q_table(-q_im).astype(BF16), lam2)


def _s5(u, tables, rt):
    n_oct, nc, b, _, _ = u.shape
    assert b == SUBLANES, "one block step of the state scan handles one sublane tile of batch rows"
    rows = nc * b
    m8, pre, pim, qre, qim, lam2 = tables
    per = lambda a, c_: pl.BlockSpec((1, a, c_), lambda o, i: (o, 0, 0))
    tile = pl.BlockSpec((1, rt, S5_ROW_W), lambda o, i: (o, i, 0))
    y = pl.pallas_call(
        _s5_kernel,
        grid=(n_oct, rows // rt),
        in_specs=[tile, per(S5_ROW_W, S5_ROW_W), per(S5_ROW_W, S5_STATE_W), per(S5_ROW_W, S5_STATE_W),
                  per(S5_STATE_W, S5_ROW_W), per(S5_STATE_W, S5_ROW_W), per(2, S5_STATE_W)],
        out_specs=tile,
        out_shape=jax.ShapeDtypeStruct((n_oct, rows, S5_ROW_W), F32),
        scratch_shapes=[pltpu.VMEM((rt, S5_STATE_W), F32)] * 2 + [pltpu.VMEM((SUBLANES, S5_STATE_W), F32)] * 2,
        compiler_params=_params("arbitrary", "arbitrary"),
        name="s5",
    )(u.reshape(n_oct, rows, S5_ROW_W), m8, pre, pim, qre, qim, lam2)
    return y.reshape(n_oct, nc, b, S5_CHUNK, LANES)


def _memkv_kernel(mem_ref, nw_ref, wk_ref, wv_ref, k_ref, v_ref):
    m = _rms(mem_ref[0], nw_ref[...]).astype(BF16)
    k_ref[0] = _dot(m, wk_ref[...]).astype(BF16)
    v_ref[0] = _dot(m, wv_ref[...]).astype(BF16)


def _memkv(mem, norm_w, w_k, w_v):
    b, nm, d = mem.shape
    const = lambda shape: pl.BlockSpec(shape, lambda i: (0,) * len(shape))
    blk = pl.BlockSpec((1, nm, d), lambda i: (i, 0, 0))
    return pl.pallas_call(
        _memkv_kernel,
        grid=(b,),
        in_specs=[blk, const((1, d)), const((d, d)), const((d, d))],
        out_specs=[blk, blk],
        out_shape=[jax.ShapeDtypeStruct((b, nm, d), BF16)] * 2,
        compiler_params=_params("arbitrary"),
        name="memkv",
    )(mem, norm_w.reshape(1, d), w_k.astype(BF16), w_v.astype(BF16))


def _post_kernel(x_ref, yr_ref, y5_ref, km_ref, vm_ref, wglu_ref, bglu_ref, wmo_ref, nx_ref,
                 wq_ref, wo_ref, nm_ref, wr_ref, br_ref, x2_ref, h3_ref, eid_ref, gate_ref):
    tm, d = x_ref.shape[1], x_ref.shape[2]
    hd = d // XATTN_HEADS
    y = jnp.concatenate([y5_ref[o].reshape(tm, LANES) for o in range(S5_N_OCT)], axis=1)
    z = 0.5 * y * (1.0 + jnp.tanh(math.sqrt(2.0 / math.pi) * (y + 0.044715 * (y * y * y))))
    glu = jax.nn.sigmoid(_dot(z.astype(BF16), wglu_ref[...]) + bglu_ref[...])
    x1 = (x_ref[0] + _dot(yr_ref[0], wmo_ref[:D_RWKV, :])
          + _dot((z * glu).astype(BF16), wmo_ref[D_RWKV:, :]))

    h = _rms(x1, nx_ref[...]).astype(BF16)
    q = (_dot(h, wq_ref[...]) * (hd ** -0.5)).astype(BF16)
    heads = []
    for i in range(XATTN_HEADS):
        sl = slice(i * hd, (i + 1) * hd)
        sc = _dot_nt(q[:, sl], km_ref[0, :, sl])
        e = jnp.exp(sc - jnp.max(sc, axis=-1, keepdims=True))
        o = _dot(e.astype(BF16), vm_ref[0, :, sl]) / jnp.sum(e, axis=-1, keepdims=True)
        heads.append(o.astype(BF16))
    x2 = x1 + _dot(jnp.concatenate(heads, axis=1), wo_ref[...])
    x2_ref[0] = x2
    h3 = _rms(x2, nm_ref[...])
    for j in range(d // LANES):
        h3_ref[_token_rows(j, tm), :] = h3[:, j * LANES:(j + 1) * LANES]

    lg = _dot3(h3, wr_ref[...]) + br_ref[...]
    lane = lax.broadcasted_iota(I32, lg.shape, 1)
    lanef = lane.astype(F32)
    neg = -jnp.inf
    no_lane = float(ROUTER_LANES)
    l1 = jnp.where(lane < N_EXPERT_GROUPS, lg, neg)
    m1 = jnp.max(l1, axis=-1, keepdims=True)
    grp = jnp.min(jnp.where(l1 == m1, lanef, no_lane), axis=-1, keepdims=True)
    g1 = 1.0 / jnp.sum(jnp.exp(l1 - m1), axis=-1, keepdims=True)
    first = N_EXPERT_GROUPS + EXPERTS_PER_GROUP * grp
    l2 = jnp.where((lanef >= first) & (lanef < first + EXPERTS_PER_GROUP), lg, neg)
    v1 = jnp.max(l2, axis=-1, keepdims=True)
    i1 = jnp.min(jnp.where(l2 == v1, lanef, no_lane), axis=-1, keepdims=True)
    l2 = jnp.where(lanef == i1, neg, l2)
    v2 = jnp.max(l2, axis=-1, keepdims=True)
    i2 = jnp.min(jnp.where(l2 == v2, lanef, no_lane), axis=-1, keepdims=True)
    e2 = jnp.exp(v2 - v1)
    den = 1.0 + e2
    eid_ref[0] = jnp.where(lane == 0, i1 - N_EXPERT_GROUPS,
                           jnp.where(lane == 1, i2 - N_EXPERT_GROUPS, 0.0)).astype(I32)
    gate_ref[0] = jnp.where(lane == 0, g1 / den, jnp.where(lane == 1, g1 * e2 / den, 0.0))


def _post(x, y_rwkv, y_s5, kmem, vmem, w_glu, b_glu, w_mix_out, norm_x, w_q, w_o, norm_moe,
          w_rg, b_rg, w_re, b_re, tm):
    b, s, d = x.shape
    assert d % LANES == 0 and d // LANES == SUBLANES, "one (8, 128) tile per token"
    nm = kmem.shape[1]
    nt = s // tm
    n_logits = N_EXPERT_GROUPS + N_EXPERTS
    w_router = jnp.concatenate(
        [w_rg, jnp.transpose(w_re, (1, 0, 2)).reshape(d, N_EXPERTS),
         jnp.zeros((d, ROUTER_LANES - n_logits), F32)], axis=1)
    b_router = jnp.concatenate([b_rg, b_re.reshape(-1), jnp.zeros((ROUTER_LANES - n_logits,), F32)])
    const = lambda shape: pl.BlockSpec(shape, lambda i, j: (0,) * len(shape))
    tok = lambda w: pl.BlockSpec((1, tm, w), lambda i, j: (i, j, 0))
    mem = pl.BlockSpec((1, nm, d), lambda i, j: (i, 0, 0))
    s5_tile = pl.BlockSpec((S5_N_OCT, tm // S5_CHUNK, None, S5_CHUNK, LANES), lambda i, j: (0, j, i, 0, 0))
    return pl.pallas_call(
        _post_kernel,
        grid=(b, nt),
        in_specs=[tok(d), tok(D_RWKV), s5_tile, mem, mem,
                  const((D_S5, D_S5)), const((1, D_S5)), const((D_RWKV + D_S5, d)), const((1, d)),
                  const((d, d)), const((d, d)), const((1, d)),
                  const((3 * d, ROUTER_LANES)), const((1, ROUTER_LANES))],
        out_specs=[tok(d), pl.BlockSpec((tm * SUBLANES, LANES), lambda i, j: (i * nt + j, 0)),
                   tok(ROUTER_LANES), tok(ROUTER_LANES)],
        out_shape=[jax.ShapeDtypeStruct((b, s, d), F32),
                   jax.ShapeDtypeStruct((b * s * SUBLANES, LANES), F32),
                   jax.ShapeDtypeStruct((b, s, ROUTER_LANES), I32),
                   jax.ShapeDtypeStruct((b, s, ROUTER_LANES), F32)],
        compiler_params=_params("arbitrary", "arbitrary"),
        name="post",
    )(x, y_rwkv, y_s5, kmem, vmem, w_glu.astype(BF16), b_glu.reshape(1, D_S5),
      w_mix_out.astype(BF16), norm_x.reshape(1, d), w_q.astype(BF16), w_o.astype(BF16),
      norm_moe.reshape(1, d), _stack3(w_router), b_router.reshape(1, ROUTER_LANES))


def _gather_pipeline(step, n_steps, idx_hbm, idx_s, src_hbm, buf, isem, rsem):
    n_tok = buf.shape[1] // SUBLANES
    slot = step % 2

    def idx_copy(t, s):
        return pltpu.make_async_copy(idx_hbm.at[t], idx_s.at[s], isem.at[s])

    def tile_copy(src_row, s, r):
        if not isinstance(src_row, int):
            src_row = pl.multiple_of(src_row, SUBLANES)
        return pltpu.make_async_copy(src_hbm.at[pl.ds(src_row, SUBLANES)],
                                     buf.at[s, pl.ds(r * SUBLANES, SUBLANES)], rsem.at[s])

    def start_tiles(s):
        for r in range(n_tok):
            tile_copy(idx_s[s, r], s, r).start()

    @pl.when(step == 0)
    def _():
        idx_copy(0, 0).start()
        idx_copy(0, 0).wait()
        start_tiles(0)

        @pl.when(n_steps > 1)
        def _():
            idx_copy(1, 1).start()

    @pl.when(step + 1 < n_steps)
    def _():
        idx_copy(step + 1, 1 - slot).wait()
        start_tiles(1 - slot)

    @pl.when(step + 2 < n_steps)
    def _():
        idx_copy(step + 2, slot).start()

    def drain():
        for r in range(n_tok):
            tile_copy(0, slot, r).wait()

    return slot, drain


def _moe_kernel(be_ref, nu_ref, tok_hbm, h_hbm, wg_ref, wu_ref, wd_ref, ys_ref,
                tok_s, buf, isem, rsem):
    del be_ref
    b = pl.program_id(0)
    n_used = nu_ref[0]
    blk = buf.shape[1] // SUBLANES
    slot, drain = _gather_pipeline(b, n_used, tok_hbm, tok_s, h_hbm, buf, isem, rsem)

    @pl.when(b < n_used)
    def _():
        drain()
        xb = jnp.concatenate([buf[slot, _token_rows(j, blk), :].astype(BF16) for j in range(SUBLANES)],
                             axis=1)
        gate = _dot(xb, wg_ref[0])
        hid = gate * jax.nn.sigmoid(gate) * _dot(xb, wu_ref[0])
        y = _dot(hid.astype(BF16), wd_ref[0])
        for j in range(SUBLANES):
            ys_ref[_token_rows(j, blk), :] = y[:, j * LANES:(j + 1) * LANES]

    @pl.when(b >= n_used)
    def _():
        ys_ref[...] = jnp.zeros_like(ys_ref)


def _moe(h3_tiles, slot_row, block_expert, n_used, w_gate, w_up, w_down):
    blk = MOE_BLOCK_ROWS
    n_blocks = slot_row.shape[0] // blk
    d, de = w_gate.shape[1], w_gate.shape[2]
    grid_spec = pltpu.PrefetchScalarGridSpec(
        num_scalar_prefetch=2,
        grid=(n_blocks,),
        in_specs=[
            pl.BlockSpec(memory_space=pl.ANY),
            pl.BlockSpec(memory_space=pl.ANY),
            pl.BlockSpec((1, d, de), lambda i, be, nu: (be[i], 0, 0)),
            pl.BlockSpec((1, d, de), lambda i, be, nu: (be[i], 0, 0)),
            pl.BlockSpec((1, de, d), lambda i, be, nu: (be[i], 0, 0)),
        ],
        out_specs=pl.BlockSpec((blk * SUBLANES, LANES), lambda i, be, nu: (i, 0)),
        scratch_shapes=[pltpu.SMEM((2, blk), I32), pltpu.VMEM((2, blk * SUBLANES, LANES), F32),
                        pltpu.SemaphoreType.DMA((2,)), pltpu.SemaphoreType.DMA((2,))],
    )
    return pl.pallas_call(
        _moe_kernel,
        grid_spec=grid_spec,
        out_shape=jax.ShapeDtypeStruct((n_blocks * blk * SUBLANES, LANES), F32),
        compiler_params=_params("arbitrary"),
        name="moe",
    )(block_expert, n_used, slot_row.reshape(n_blocks, blk), h3_tiles,
      w_gate.astype(BF16), w_up.astype(BF16), w_down.astype(BF16))


def _route_tables(eid, blk):
    n = eid.shape[0]
    n_assign = n * TOP_K
    flat_e = eid.reshape(-1)
    onehot = (flat_e[:, None] == jnp.arange(N_EXPERTS, dtype=I32)[None, :]).astype(I32)
    csum = jnp.cumsum(onehot, axis=0)
    rank = jnp.sum(csum * onehot, axis=1) - 1
    counts = csum[-1]
    padded = (counts + blk - 1) // blk * blk
    pad_end = jnp.cumsum(padded)
    dest = (pad_end - padded)[flat_e] + rank
    n_blocks = -(-n_assign // blk) + N_EXPERTS
    tok_row = jnp.arange(n_assign, dtype=I32) // TOP_K * SUBLANES
    slot_row = jnp.zeros((n_blocks * blk,), I32).at[dest].set(tok_row, unique_indices=True)
    block_start = jnp.arange(n_blocks, dtype=I32) * blk
    block_expert = jnp.minimum(jnp.sum((pad_end[None, :] <= block_start[:, None]).astype(I32), axis=1),
                               N_EXPERTS - 1)
    n_used = (pad_end[-1:] // blk).astype(I32)
    return slot_row, block_expert, n_used, (dest * SUBLANES).reshape(n, TOP_K).astype(I32)


def _combine_kernel(pos_hbm, x2_ref, gate_ref, nw_ref, ys_hbm, out_ref, pos_s, buf, isem, rsem):
    tm = x2_ref.shape[0]
    slot, drain = _gather_pipeline(pl.program_id(0), pl.num_programs(0), pos_hbm, pos_s, ys_hbm,
                                   buf, isem, rsem)
    drain()
    g = gate_ref[...]
    g0, g1 = g[:, 0:1], g[:, 1:2]
    y = jnp.concatenate(
        [g0 * buf[slot, _token_rows(j, tm), :] + g1 * buf[slot, _token_rows(tm * SUBLANES + j, tm), :]
         for j in range(SUBLANES)], axis=1)
    out_ref[...] = _rms(x2_ref[...] + y, nw_ref[...])


def _combine(x2, gates, ys_tiles, pos_row, norm_w, tm):
    n, d = x2.shape
    nt = n // tm
    pos_t = jnp.transpose(pos_row.reshape(nt, tm, TOP_K), (0, 2, 1)).reshape(nt, TOP_K * tm)
    return pl.pallas_call(
        _combine_kernel,
        grid=(nt,),
        in_specs=[
            pl.BlockSpec(memory_space=pl.ANY),
            pl.BlockSpec((tm, d), lambda i: (i, 0)),
            pl.BlockSpec((tm, ROUTER_LANES), lambda i: (i, 0)),
            pl.BlockSpec((1, d), lambda i: (0, 0)),
            pl.BlockSpec(memory_space=pl.ANY),
        ],
        out_specs=pl.BlockSpec((tm, d), lambda i: (i, 0)),
        out_shape=jax.ShapeDtypeStruct((n, d), F32),
        scratch_shapes=[pltpu.SMEM((2, TOP_K * tm), I32),
                        pltpu.VMEM((2, TOP_K * tm * SUBLANES, LANES), F32),
                        pltpu.SemaphoreType.DMA((2,)), pltpu.SemaphoreType.DMA((2,))],
        compiler_params=_params("arbitrary"),
        name="combine",
    )(pos_t, x2, gates, norm_w.reshape(1, d), ys_tiles)


def _tile(s, want):
    return min(s, want)


def kernel(x, mem, norm_mix_w, w_in, mu_shift, w0, w_decay_up, a0, w_iclr_up, w_gate_up, k_k, k_a, r_k, ln_x_w, ln_x_b, s5_lam_re, s5_lam_im, s5_log_dt, s5_b_re, s5_b_im, s5_c_re, s5_c_im, s5_d, w_glu, b_glu, w_mix_out, norm_xattn_w, norm_mem_w, w_q, w_k, w_v, w_o, norm_moe_w, w_router_group, b_router_group, w_router_expert, b_router_expert, w_exp_gate, w_exp_up, w_exp_down, norm_final_w):
    b, s, d = x.shape
    assert norm_mix_w.shape[0] == 1, "one layer"
    p, u = _inproj(x, norm_mix_w[0], w_in[0], mu_shift[0], _tile(s, 512))
    y_rwkv = _rwkv(p, w0[0], w_decay_up[0], a0[0], w_iclr_up[0], w_gate_up[0], k_k[0], k_a[0],
                   r_k[0], ln_x_w[0], ln_x_b[0], _tile(s, 256))
    y_s5 = _s5(u, _s5_tables(s5_lam_re[0], s5_lam_im[0], s5_log_dt[0], s5_b_re[0], s5_b_im[0],
                             s5_c_re[0], s5_c_im[0], s5_d[0]), _tile(s // S5_CHUNK * b, 512))
    kmem, vmem = _memkv(mem, norm_mem_w[0], w_k[0], w_v[0])
    x2, h3_tiles, eid, gates = _post(x, y_rwkv, y_s5, kmem, vmem, w_glu[0], b_glu[0], w_mix_out[0],
                                     norm_xattn_w[0], w_q[0], w_o[0], norm_moe_w[0],
                                     w_router_group[0], b_router_group[0], w_router_expert[0],
                                     b_router_expert[0], _tile(s, 256))
    n = b * s
    eid = eid.reshape(n, ROUTER_LANES)[:, :TOP_K]
    slot_row, block_expert, n_used, pos_row = _route_tables(eid, MOE_BLOCK_ROWS)
    ys_tiles = _moe(h3_tiles, slot_row, block_expert, n_used, w_exp_gate[0], w_exp_up[0], w_exp_down[0])
    out = _combine(x2.reshape(n, d), gates.reshape(n, ROUTER_LANES), ys_tiles, pos_row,
                   norm_final_w, _tile(n, 256))
    return out.reshape(b, s, d)
```

```python
import math

import jax
import jax.numpy as jnp
from jax import lax
from jax.experimental import pallas as pl
from jax.experimental.pallas import tpu as pltpu

F32 = jnp.float32
BF16 = jnp.bfloat16
I32 = jnp.int32

HEAD_DIM = 64
D_RWKV = 512
DECAY_RANK = 64
ICLR_RANK = 64
GATE_RANK = 128
D_SHIFT = 3 * D_RWKV + DECAY_RANK + ICLR_RANK + GATE_RANK
S5_CH = 16
S5_STATE = 64
D_S5 = 512
S5_GROUPS = D_S5 // S5_CH
XATTN_HEADS = 4
N_EXPERT_GROUPS = 4
EXPERTS_PER_GROUP = 8
N_EXPERTS = N_EXPERT_GROUPS * EXPERTS_PER_GROUP
TOP_K = 2
RMS_EPS = 1e-6
GN_EPS = 64e-5
L2_EPS = 1e-12

LANES = 128
SUBLANES = 8
VMEM_LIMIT_BYTES = 56 * 1024 * 1024

RWKV_CHUNK = 64
RWKV_CHUNKS_PER_ITER = 2
PAIR = 2 * HEAD_DIM
S5_CHUNK = 16
S5_OCT = LANES // S5_CH
S5_N_OCT = S5_GROUPS // S5_OCT
S5_ROW_W = S5_CHUNK * LANES
S5_STATE_W = S5_OCT * S5_STATE
MOE_BLOCK_ROWS = 256
ROUTER_LANES = 128


def _dot(a, b):
    return jnp.dot(a, b, preferred_element_type=F32)


def _dot_nt(a, b):
    return lax.dot_general(a, b, (((1,), (1,)), ((), ())), preferred_element_type=F32)


def _dot_tn(a, b):
    return lax.dot_general(a, b, (((0,), (0,)), ((), ())), preferred_element_type=F32)


def _split2(x):
    hi = x.astype(BF16)
    lo = (x - hi.astype(F32)).astype(BF16)
    return hi, lo


def _split3(x):
    hi = x.astype(BF16)
    r = x - hi.astype(F32)
    mid = r.astype(BF16)
    lo = (r - mid.astype(F32)).astype(BF16)
    return hi, mid, lo


def _stack3(w):
    hi, lo = _split2(w.astype(F32))
    return jnp.concatenate([hi, hi, lo], axis=0)


def _dot3(x, w3):
    hi, lo = _split2(x)
    return _dot(jnp.concatenate([hi, lo, hi], axis=1), w3)


def _rms(x, w):
    return x * lax.rsqrt(jnp.mean(x * x, axis=-1, keepdims=True) + RMS_EPS) * w


def _softplus(x):
    return jnp.maximum(x, 0.0) + jnp.log1p(jnp.exp(-jnp.abs(x)))


def _params(*sem):
    return pltpu.CompilerParams(dimension_semantics=sem, vmem_limit_bytes=VMEM_LIMIT_BYTES)


def _token_rows(j, n):
    return pl.ds(j, n, stride=SUBLANES)


def _inproj_kernel(x_ref, nw_ref, w_ref, mu_ref, p_ref, u_ref, carry_ref):
    tm = x_ref.shape[1]

    @pl.when(pl.program_id(1) == 0)
    def _():
        carry_ref[...] = jnp.zeros_like(carry_ref)

    h = _rms(x_ref[0], nw_ref[...]).astype(BF16)
    proj = _dot(h, w_ref[...])
    z = proj[:, :D_SHIFT]
    row = lax.broadcasted_iota(I32, z.shape, 0)
    prev = jnp.where(row == 0, carry_ref[0:1, :], pltpu.roll(z, 1, 0))
    carry_ref[0:1, :] = z[tm - 1:tm, :]
    p_ref[0] = z + (prev - z) * mu_ref[...]
    for o in range(S5_N_OCT):
        u = proj[:, D_SHIFT + o * LANES:D_SHIFT + (o + 1) * LANES].astype(BF16)
        u_ref[o] = u.reshape(tm // S5_CHUNK, S5_CHUNK, LANES)


def _inproj(x, norm_w, w_in, mu, tm):
    b, s, d = x.shape
    d_in = w_in.shape[1]
    assert d_in - D_SHIFT == D_S5
    return pl.pallas_call(
        _inproj_kernel,
        grid=(b, s // tm),
        in_specs=[
            pl.BlockSpec((1, tm, d), lambda i, j: (i, j, 0)),
            pl.BlockSpec((1, d), lambda i, j: (0, 0)),
            pl.BlockSpec((d, d_in), lambda i, j: (0, 0)),
            pl.BlockSpec((1, D_SHIFT), lambda i, j: (0, 0)),
        ],
        out_specs=[
            pl.BlockSpec((1, tm, D_SHIFT), lambda i, j: (i, j, 0)),
            pl.BlockSpec((S5_N_OCT, tm // S5_CHUNK, None, S5_CHUNK, LANES), lambda i, j: (0, j, i, 0, 0)),
        ],
        out_shape=[
            jax.ShapeDtypeStruct((b, s, D_SHIFT), F32),
            jax.ShapeDtypeStruct((S5_N_OCT, s // S5_CHUNK, b, S5_CHUNK, LANES), BF16),
        ],
        scratch_shapes=[pltpu.VMEM((SUBLANES, D_SHIFT), F32)],
        compiler_params=_params("arbitrary", "arbitrary"),
        name="inproj",
    )(x, norm_w.reshape(1, d), w_in.astype(BF16), mu.reshape(1, D_SHIFT))


def _headsum(x, ones2):
    hi, lo = _split2(x)
    return _dot(jnp.concatenate([hi, lo], axis=1), ones2)


def _rwkv_kernel(p_ref, wcat_ref, wg_ref, vec_ref, tri_ref, ones2_ref, y_ref,
                 lw_s, cum_s, a_s, g_s, h_s):
    t2 = p_ref.shape[1]
    c = RWKV_CHUNK
    n_pairs = D_RWKV // PAIR

    @pl.when(pl.program_id(1) == 0)
    def _():
        h_s[...] = jnp.zeros_like(h_s)

    lane_t = lax.broadcasted_iota(I32, (t2, LANES), 1)
    lo_col = 3 * D_RWKV
    x = p_ref[0, :, lo_col:lo_col + LANES]
    zz = _dot3(jnp.where(lane_t < DECAY_RANK, jnp.tanh(x), x), wcat_ref[...])
    lw = -jnp.exp(-_softplus(-(vec_ref[0:1, :] + zz[:, :D_RWKV])) - 0.5)
    lw_s[...] = lw
    hi, mid, lo = _split3(lw)
    c3 = _dot(tri_ref[...], jnp.concatenate([hi, mid, lo], axis=1))
    cum_s[...] = c3[:, :D_RWKV] + c3[:, D_RWKV:2 * D_RWKV] + c3[:, 2 * D_RWKV:]
    a_s[...] = jax.nn.sigmoid(vec_ref[1:2, :] + zz[:, D_RWKV:])
    dg = p_ref[0, :, lo_col + LANES:lo_col + 2 * LANES]
    g_s[...] = _dot(jax.nn.sigmoid(dg).astype(BF16), wg_ref[...])

    t_idx = lax.broadcasted_iota(I32, (c, PAIR), 0)
    lane = lax.broadcasted_iota(I32, (c, PAIR), 1)
    s_idx = lane & (HEAD_DIM - 1)
    mask_strict = s_idx < t_idx
    mask_incl = s_idx <= t_idx
    eye_lane = s_idx == t_idx
    head0 = lane < HEAD_DIM
    row_b = lax.broadcasted_iota(I32, (PAIR, PAIR), 0)
    col_b = lax.broadcasted_iota(I32, (PAIR, PAIR), 1)
    eye_block = row_b == col_b
    ones2 = ones2_ref[...]

    def block(v, dtype=BF16):
        return jnp.concatenate([jnp.where(head0, v, 0.0), jnp.where(head0, 0.0, v)], axis=0).astype(dtype)

    def cat0(a, b):
        return jnp.concatenate([a, b], axis=0)

    def body(it, carry):
        streams = [(ci, j) for ci in range(RWKV_CHUNKS_PER_ITER) for j in range(n_pairs)]
        ns = range(len(streams))
        rows_of = [pl.ds(pl.multiple_of((it * RWKV_CHUNKS_PER_ITER + ci) * c, c), c)
                   for ci in range(RWKV_CHUNKS_PER_ITER)]

        def ld(ref, s):
            ci, j = streams[s]
            return ref[rows_of[ci], j * PAIR:(j + 1) * PAIR]

        def ldp(s, off):
            ci, j = streams[s]
            return p_ref[0, rows_of[ci], off + j * PAIR:off + (j + 1) * PAIR]

        def vec(i, s):
            j = streams[s][1]
            return vec_ref[i:i + 1, j * PAIR:(j + 1) * PAIR]

        r = [ldp(s, 0) for s in ns]
        k = [ldp(s, D_RWKV) for s in ns]
        v = [ldp(s, 2 * D_RWKV) for s in ns]
        kk = [k[s] * vec(2, s) for s in ns]
        ss = [_headsum(kk[s] * kk[s], ones2) for s in ns]
        a_blk, v_blk, bk_end, bc_blk, km, w_end, r_t, lhs, rhs = [], [], [], [], [], [], [], [], []
        for s in ns:
            lw_c, cum, a = ld(lw_s, s), ld(cum_s, s), ld(a_s, s)
            kkn = kk[s] / jnp.maximum(jnp.sqrt(ss[s]), L2_EPS)
            km_s = k[s] * (1.0 + (a - 1.0) * vec(3, s))
            beta = kkn * a
            cum_end = cum[c - 1:c, :]
            e_neg = jnp.exp(-cum)
            e_end = jnp.exp(cum_end - cum)
            a_t = -kkn * jnp.exp(cum - lw_c)
            r_f = r[s] * jnp.exp(cum)
            bc = block(beta * e_end)
            km.append(km_s)
            w_end.append(jnp.exp(cum_end))
            r_t.append(r_f)
            a_blk.append(block(a_t))
            v_blk.append(block(v[s]))
            bc_blk.append(bc)
            bk_end.append(cat0(bc, block(km_s * e_end)))
            lhs.append(cat0(a_t, r_f).astype(BF16))
            rhs.append(cat0(block(beta * e_neg), block(km_s * e_neg)))

        gram = [_dot_nt(lhs[s], rhs[s]) for s in ns]
        l_ab = [jnp.where(mask_strict, gram[s][:c, :PAIR], 0.0) for s in ns]
        l_ak = [jnp.where(mask_strict, gram[s][:c, PAIR:], 0.0).astype(BF16) for s in ns]
        m_cat = [jnp.concatenate([jnp.where(mask_incl, gram[s][c:, :PAIR], 0.0),
                                  jnp.where(mask_incl, gram[s][c:, PAIR:], 0.0)],
                                 axis=1).astype(BF16) for s in ns]

        inv = [jnp.where(eye_lane, 1.0, l_ab[s]) for s in ns]
        lp = l_ab
        lp_blk = [block(lp[s]) for s in ns]
        n = 1
        while 2 * n < c:
            lp = [_dot(lp[s].astype(BF16), lp_blk[s]) for s in ns]
            lp_blk = [block(lp[s]) for s in ns]
            inv = [inv[s] + _dot(inv[s].astype(BF16), lp_blk[s]) for s in ns]
            n *= 2
        inv = [inv[s].astype(BF16) for s in ns]

        a_p = [block(_dot(inv[s], a_blk[s])) for s in ns]
        w1 = [block(_dot(l_ak[s], v_blk[s])) for s in ns]
        uv = [cat0(block(_dot(inv[s], w1[s])), v_blk[s]) for s in ns]
        r_p = [(r_t[s] + _dot(m_cat[s][:, :PAIR], a_p[s])).astype(BF16) for s in ns]
        y_p = [_dot(m_cat[s], uv[s]) for s in ns]
        g_t = [(jnp.where(eye_block, w_end[s], 0.0) + _dot_tn(a_p[s], bc_blk[s])).astype(BF16)
               for s in ns]
        h_p = [_dot_tn(uv[s], bk_end[s]) for s in ns]

        state = [h_s[j] for j in range(n_pairs)]
        y = [None] * len(streams)
        for ci in range(RWKV_CHUNKS_PER_ITER):
            base = ci * n_pairs
            for j in range(n_pairs):
                y[base + j] = _dot_nt(r_p[base + j], state[j].astype(BF16)) + y_p[base + j]
            s_lane = [(state[j][:c] + state[j][c:]).astype(BF16) for j in range(n_pairs)]
            state = [block(_dot(s_lane[j], g_t[base + j]), F32) + h_p[base + j]
                     for j in range(n_pairs)]
        for j in range(n_pairs):
            h_s[j] = state[j]

        mu = [_headsum(y[s], ones2) * (1.0 / HEAD_DIM) for s in ns]
        d = [y[s] - mu[s] for s in ns]
        var = [_headsum(d[s] * d[s], ones2) * (1.0 / HEAD_DIM) for s in ns]
        bonus = [_headsum(r[s] * km[s] * vec(4, s), ones2) for s in ns]
        for s in ns:
            ci, j = streams[s]
            yn = d[s] * lax.rsqrt(var[s] + GN_EPS) * vec(5, s) + vec(6, s)
            y_ref[0, rows_of[ci], j * PAIR:(j + 1) * PAIR] = (
                (yn + bonus[s] * v[s]) * ld(g_s, s)).astype(BF16)
        return carry

    lax.fori_loop(0, t2 // (c * RWKV_CHUNKS_PER_ITER), body, 0)


def _rwkv(p, w0, w_decay_up, a0, w_iclr_up, w_gate_up, k_k, k_a, r_k, ln_w, ln_b, t2):
    b, s, _ = p.shape
    zeros = jnp.zeros((DECAY_RANK, D_RWKV), F32)
    wcat = jnp.concatenate([jnp.concatenate([w_decay_up, zeros], axis=1),
                            jnp.concatenate([zeros, w_iclr_up], axis=1)], axis=0)
    vec = jnp.stack([w0, a0, k_k, k_a, r_k.reshape(-1), ln_w, ln_b, jnp.zeros_like(w0)])
    ti = jnp.arange(t2)
    tri = ((ti[:, None] // RWKV_CHUNK == ti[None, :] // RWKV_CHUNK) & (ti[None, :] <= ti[:, None])).astype(BF16)
    li = jnp.arange(PAIR)
    ones_bd = (li[:, None] // HEAD_DIM == li[None, :] // HEAD_DIM).astype(BF16)
    ones2 = jnp.concatenate([ones_bd, ones_bd], axis=0)
    const = lambda shape: pl.BlockSpec(shape, lambda i, j: (0,) * len(shape))
    return pl.pallas_call(
        _rwkv_kernel,
        grid=(b, s // t2),
        in_specs=[
            pl.BlockSpec((1, t2, D_SHIFT), lambda i, j: (i, j, 0)),
            const((3 * LANES, 2 * D_RWKV)),
            const((GATE_RANK, D_RWKV)),
            const((SUBLANES, D_RWKV)),
            const((t2, t2)),
            const((2 * PAIR, PAIR)),
        ],
        out_specs=pl.BlockSpec((1, t2, D_RWKV), lambda i, j: (i, j, 0)),
        out_shape=jax.ShapeDtypeStruct((b, s, D_RWKV), BF16),
        scratch_shapes=[pltpu.VMEM((t2, D_RWKV), F32)] * 4
        + [pltpu.VMEM((D_RWKV // PAIR, PAIR, PAIR), F32)],
        compiler_params=_params("arbitrary", "arbitrary"),
        name="rwkv",
    )(p, _stack3(wcat), w_gate_up.astype(BF16), vec, tri, ones2)


def _s5_kernel(u_ref, m_ref, pre_ref, pim_ref, qre_ref, qim_ref, lam_ref, y_ref, xr_s, xi_s, cr_s, ci_s):
    nb = SUBLANES

    @pl.when(pl.program_id(1) == 0)
    def _():
        cr_s[...] = jnp.zeros_like(cr_s)
        ci_s[...] = jnp.zeros_like(ci_s)

    u = u_ref[0]
    xr_s[...] = _dot(u, pre_ref[0])
    xi_s[...] = _dot(u, pim_ref[0])
    lr = jnp.broadcast_to(lam_ref[0, 0:1, :], (nb, S5_STATE_W))
    li = jnp.broadcast_to(lam_ref[0, 1:2, :], (nb, S5_STATE_W))

    def step(ci, carry):
        xr, xi = carry
        rows = pl.ds(pl.multiple_of(ci * nb, nb), nb)
        ar = xr_s[rows, :]
        ai = xi_s[rows, :]
        xr_s[rows, :] = xr
        xi_s[rows, :] = xi
        return lr * xr - li * xi + ar, lr * xi + li * xr + ai

    xr, xi = lax.fori_loop(0, u.shape[0] // nb, step, (cr_s[...], ci_s[...]), unroll=8)
    cr_s[...] = xr
    ci_s[...] = xi
    y = (_dot(u, m_ref[0]) + _dot(xr_s[...].astype(BF16), qre_ref[0])
         + _dot(xi_s[...].astype(BF16), qim_ref[0]))
    for t in range(S5_CHUNK):
        y_ref[0, pl.ds(t, u.shape[0], stride=S5_CHUNK), :] = y[:, t * LANES:(t + 1) * LANES]


def _s5_tables(lam_re, lam_im, log_dt, b_re, b_im, c_re, c_im, d_skip):
    tc = S5_CHUNK
    dt = jnp.exp(log_dt)[:, None]
    ar, ai = lam_re * dt, lam_im * dt
    taus = jnp.arange(tc + 1, dtype=F32)[:, None, None]
    mag = jnp.exp(ar[None] * taus)
    pw_re, pw_im = mag * jnp.cos(ai[None] * taus), mag * jnp.sin(ai[None] * taus)
    x_re, x_im = pw_re[1] - 1.0, pw_im[1]
    den = lam_re * lam_re + lam_im * lam_im
    co_re = (x_re * lam_re + x_im * lam_im) / den
    co_im = (x_im * lam_re - x_re * lam_im) / den
    bb_re = co_re[..., None] * b_re - co_im[..., None] * b_im
    bb_im = co_re[..., None] * b_im + co_im[..., None] * b_re
    w_re = pw_re[:tc, :, :, None] * bb_re[None] - pw_im[:tc, :, :, None] * bb_im[None]
    w_im = pw_re[:tc, :, :, None] * bb_im[None] + pw_im[:tc, :, :, None] * bb_re[None]
    kern = jnp.einsum('gjn,tgni->tgji', c_re, w_re) - jnp.einsum('gjn,tgni->tgji', c_im, w_im)
    kern = kern.at[0].add(d_skip[:, :, None] * jnp.eye(S5_CH, dtype=F32))
    lag = jnp.arange(tc)[None, :] - jnp.arange(tc)[:, None]
    kst = jnp.where((lag >= 0)[:, :, None, None, None], kern[jnp.clip(lag, 0)], 0.0)
    eye = jnp.eye(S5_OCT, dtype=F32)
    k6 = kst.reshape(tc, tc, S5_N_OCT, S5_OCT, S5_CH, S5_CH)
    m8 = jnp.einsum('stogji,gh->osgithj', k6, eye).reshape(S5_N_OCT, S5_ROW_W, S5_ROW_W)
    flip = tc - 1 - jnp.arange(tc)

    def p_table(w):
        p6 = w[flip].reshape(tc, S5_N_OCT, S5_OCT, S5_STATE, S5_CH)
        return jnp.einsum('sogni,gh->osgihn', p6, eye).reshape(S5_N_OCT, S5_ROW_W, S5_STATE_W)

    q_re = c_re[None] * pw_re[1:, :, None, :] - c_im[None] * pw_im[1:, :, None, :]
    q_im = c_re[None] * pw_im[1:, :, None, :] + c_im[None] * pw_re[1:, :, None, :]

    def q_table(q):
        q6 = q.reshape(tc, S5_N_OCT, S5_OCT, S5_CH, S5_STATE)
        return jnp.einsum('togjn,gh->ognthj', q6, eye).reshape(S5_N_OCT, S5_STATE_W, S5_ROW_W)

    lam2 = jnp.stack([pw_re[tc].reshape(S5_N_OCT, S5_STATE_W), pw_im[tc].reshape(S5_N_OCT, S5_STATE_W)], axis=1)
    return (m8.astype(BF16), p_table(w_re).astype(BF16), p_table(w_im).astype(BF16),
            q_table(q_re).astype(BF16), q_table(-q_im).astype(BF16), lam2)


def _s5(u, tables, rt):
    n_oct, nc, b, _, _ = u.shape
    assert b == SUBLANES, "one block step of the state scan handles one sublane tile of batch rows"
    rows = nc * b
    m8, pre, pim, qre, qim, lam2 = tables
    per = lambda a, c_: pl.BlockSpec((1, a, c_), lambda o, i: (o, 0, 0))
    tile = pl.BlockSpec((1, rt, S5_ROW_W), lambda o, i: (o, i, 0))
    y = pl.pallas_call(
        _s5_kernel,
        grid=(n_oct, rows // rt),
        in_specs=[tile, per(S5_ROW_W, S5_ROW_W), per(S5_ROW_W, S5_STATE_W), per(S5_ROW_W, S5_STATE_W),
                  per(S5_STATE_W, S5_ROW_W), per(S5_STATE_W, S5_ROW_W), per(2, S5_STATE_W)],
        out_specs=pl.BlockSpec((1, rt * S5_CHUNK, LANES), lambda o, i: (o, i, 0)),
        out_shape=jax.ShapeDtypeStruct((n_oct, rows * S5_CHUNK, LANES), F32),
        scratch_shapes=[pltpu.VMEM((rt, S5_STATE_W), F32)] * 2 + [pltpu.VMEM((SUBLANES, S5_STATE_W), F32)] * 2,
        compiler_params=_params("arbitrary", "arbitrary"),
        name="s5",
    )(u.reshape(n_oct, rows, S5_ROW_W), m8, pre, pim, qre, qim, lam2)
    return y.reshape(n_oct, nc, b, S5_CHUNK, LANES)


def _memkv_kernel(mem_ref, nw_ref, wk_ref, wv_ref, k_ref, v_ref):
    m = _rms(mem_ref[0], nw_ref[...]).astype(BF16)
    k_ref[0] = _dot(m, wk_ref[...]).astype(BF16)
    v_ref[0] = _dot(m, wv_ref[...]).astype(BF16)


def _memkv(mem, norm_w, w_k, w_v):
    b, nm, d = mem.shape
    const = lambda shape: pl.BlockSpec(shape, lambda i: (0,) * len(shape))
    blk = pl.BlockSpec((1, nm, d), lambda i: (i, 0, 0))
    return pl.pallas_call(
        _memkv_kernel,
        grid=(b,),
        in_specs=[blk, const((1, d)), const((d, d)), const((d, d))],
        out_specs=[blk, blk],
        out_shape=[jax.ShapeDtypeStruct((b, nm, d), BF16)] * 2,
        compiler_params=_params("arbitrary"),
        name="memkv",
    )(mem, norm_w.reshape(1, d), w_k.astype(BF16), w_v.astype(BF16))


def _post_kernel(x_ref, yr_ref, y5_ref, km_ref, vm_ref, wglu_ref, bglu_ref, wmo_ref, nx_ref,
                 wq_ref, wo_ref, nm_ref, wr_ref, br_ref, x2_ref, h3_ref, eid_ref, gate_ref):
    tm, d = x_ref.shape[1], x_ref.shape[2]
    hd = d // XATTN_HEADS
    y = jnp.concatenate([y5_ref[o].reshape(tm, LANES) for o in range(S5_N_OCT)], axis=1)
    z = 0.5 * y * (1.0 + jnp.tanh(math.sqrt(2.0 / math.pi) * (y + 0.044715 * (y * y * y))))
    glu = jax.nn.sigmoid(_dot(z.astype(BF16), wglu_ref[...]) + bglu_ref[...])
    x1 = (x_ref[0] + _dot(yr_ref[0], wmo_ref[:D_RWKV, :])
          + _dot((z * glu).astype(BF16), wmo_ref[D_RWKV:, :]))

    h = _rms(x1, nx_ref[...]).astype(BF16)
    q = (_dot(h, wq_ref[...]) * (hd ** -0.5)).astype(BF16)
    heads = []
    for i in range(XATTN_HEADS):
        sl = slice(i * hd, (i + 1) * hd)
        sc = _dot_nt(q[:, sl], km_ref[0, :, sl])
        e = jnp.exp(sc - jnp.max(sc, axis=-1, keepdims=True))
        o = _dot(e.astype(BF16), vm_ref[0, :, sl]) / jnp.sum(e, axis=-1, keepdims=True)
        heads.append(o.astype(BF16))
    x2 = x1 + _dot(jnp.concatenate(heads, axis=1), wo_ref[...])
    x2_ref[0] = x2
    h3 = _rms(x2, nm_ref[...])
    for j in range(d // LANES):
        h3_ref[_token_rows(j, tm), :] = h3[:, j * LANES:(j + 1) * LANES]

    lg = _dot3(h3, wr_ref[...]) + br_ref[...]
    lane = lax.broadcasted_iota(I32, lg.shape, 1)
    lanef = lane.astype(F32)
    neg = -jnp.inf
    no_lane = float(ROUTER_LANES)
    l1 = jnp.where(lane < N_EXPERT_GROUPS, lg, neg)
    m1 = jnp.max(l1, axis=-1, keepdims=True)
    grp = jnp.min(jnp.where(l1 == m1, lanef, no_lane), axis=-1, keepdims=True)
    g1 = 1.0 / jnp.sum(jnp.exp(l1 - m1), axis=-1, keepdims=True)
    first = N_EXPERT_GROUPS + EXPERTS_PER_GROUP * grp
    l2 = jnp.where((lanef >= first) & (lanef < first + EXPERTS_PER_GROUP), lg, neg)
    v1 = jnp.max(l2, axis=-1, keepdims=True)
    i1 = jnp.min(jnp.where(l2 == v1, lanef, no_lane), axis=-1, keepdims=True)
    l2 = jnp.where(lanef == i1, neg, l2)
    v2 = jnp.max(l2, axis=-1, keepdims=True)
    i2 = jnp.min(jnp.where(l2 == v2, lanef, no_lane), axis=-1, keepdims=True)
    e2 = jnp.exp(v2 - v1)
    den = 1.0 + e2
    eid_ref[0] = jnp.where(lane == 0, i1 - N_EXPERT_GROUPS,
                           jnp.where(lane == 1, i2 - N_EXPERT_GROUPS, 0.0)).astype(I32)
    gate_ref[0] = jnp.where(lane == 0, g1 / den, jnp.where(lane == 1, g1 * e2 / den, 0.0))


def _post(x, y_rwkv, y_s5, kmem, vmem, w_glu, b_glu, w_mix_out, norm_x, w_q, w_o, norm_moe,
          w_rg, b_rg, w_re, b_re, tm):
    b, s, d = x.shape
    assert d % LANES == 0 and d // LANES == SUBLANES, "one (8, 128) tile per token"
    nm = kmem.shape[1]
    nt = s // tm
    n_logits = N_EXPERT_GROUPS + N_EXPERTS
    w_router = jnp.concatenate(
        [w_rg, jnp.transpose(w_re, (1, 0, 2)).reshape(d, N_EXPERTS),
         jnp.zeros((d, ROUTER_LANES - n_logits), F32)], axis=1)
    b_router = jnp.concatenate([b_rg, b_re.reshape(-1), jnp.zeros((ROUTER_LANES - n_logits,), F32)])
    const = lambda shape: pl.BlockSpec(shape, lambda i, j: (0,) * len(shape))
    tok = lambda w: pl.BlockSpec((1, tm, w), lambda i, j: (i, j, 0))
    mem = pl.BlockSpec((1, nm, d), lambda i, j: (i, 0, 0))
    s5_tile = pl.BlockSpec((S5_N_OCT, tm // S5_CHUNK, None, S5_CHUNK, LANES), lambda i, j: (0, j, i, 0, 0))
    return pl.pallas_call(
        _post_kernel,
        grid=(b, nt),
        in_specs=[tok(d), tok(D_RWKV), s5_tile, mem, mem,
                  const((D_S5, D_S5)), const((1, D_S5)), const((D_RWKV + D_S5, d)), const((1, d)),
                  const((d, d)), const((d, d)), const((1, d)),
                  const((3 * d, ROUTER_LANES)), const((1, ROUTER_LANES))],
        out_specs=[tok(d), pl.BlockSpec((tm * SUBLANES, LANES), lambda i, j: (i * nt + j, 0)),
                   tok(ROUTER_LANES), tok(ROUTER_LANES)],
        out_shape=[jax.ShapeDtypeStruct((b, s, d), F32),
                   jax.ShapeDtypeStruct((b * s * SUBLANES, LANES), F32),
                   jax.ShapeDtypeStruct((b, s, ROUTER_LANES), I32),
                   jax.ShapeDtypeStruct((b, s, ROUTER_LANES), F32)],
        compiler_params=_params("arbitrary", "arbitrary"),
        name="post",
    )(x, y_rwkv, y_s5, kmem, vmem, w_glu.astype(BF16), b_glu.reshape(1, D_S5),
      w_mix_out.astype(BF16), norm_x.reshape(1, d), w_q.astype(BF16), w_o.astype(BF16),
      norm_moe.reshape(1, d), _stack3(w_router), b_router.reshape(1, ROUTER_LANES))


def _gather_pipeline(step, n_steps, idx_hbm, idx_s, src_hbm, buf, isem, rsem):
    n_tok = buf.shape[1] // SUBLANES
    slot = step % 2

    def idx_copy(t, s):
        return pltpu.make_async_copy(idx_hbm.at[t], idx_s.at[s], isem.at[s])

    def tile_copy(src_row, s, r):
        if not isinstance(src_row, int):
            src_row = pl.multiple_of(src_row, SUBLANES)
        return pltpu.make_async_copy(src_hbm.at[pl.ds(src_row, SUBLANES)],
                                     buf.at[s, pl.ds(r * SUBLANES, SUBLANES)], rsem.at[s])

    def start_tiles(s):
        for r in range(n_tok):
            tile_copy(idx_s[s, r], s, r).start()

    @pl.when(step == 0)
    def _():
        idx_copy(0, 0).start()
        idx_copy(0, 0).wait()
        start_tiles(0)

        @pl.when(n_steps > 1)
        def _():
            idx_copy(1, 1).start()

    @pl.when(step + 1 < n_steps)
    def _():
        idx_copy(step + 1, 1 - slot).wait()
        start_tiles(1 - slot)

    @pl.when(step + 2 < n_steps)
    def _():
        idx_copy(step + 2, slot).start()

    def drain():
        for r in range(n_tok):
            tile_copy(0, slot, r).wait()

    return slot, drain


def _moe_kernel(be_ref, nu_ref, tok_hbm, h_hbm, wg_ref, wu_ref, wd_ref, ys_ref,
                tok_s, buf, isem, rsem):
    del be_ref
    b = pl.program_id(0)
    n_used = nu_ref[0]
    blk = buf.shape[1] // SUBLANES
    slot, drain = _gather_pipeline(b, n_used, tok_hbm, tok_s, h_hbm, buf, isem, rsem)

    @pl.when(b < n_used)
    def _():
        drain()
        xb = jnp.concatenate([buf[slot, _token_rows(j, blk), :].astype(BF16) for j in range(SUBLANES)],
                             axis=1)
        gate = _dot(xb, wg_ref[0])
        hid = gate * jax.nn.sigmoid(gate) * _dot(xb, wu_ref[0])
        y = _dot(hid.astype(BF16), wd_ref[0])
        for j in range(SUBLANES):
            ys_ref[_token_rows(j, blk), :] = y[:, j * LANES:(j + 1) * LANES]

    @pl.when(b >= n_used)
    def _():
        ys_ref[...] = jnp.zeros_like(ys_ref)


def _moe(h3_tiles, slot_row, block_expert, n_used, w_gate, w_up, w_down):
    blk = MOE_BLOCK_ROWS
    n_blocks = slot_row.shape[0] // blk
    d, de = w_gate.shape[1], w_gate.shape[2]
    grid_spec = pltpu.PrefetchScalarGridSpec(
        num_scalar_prefetch=2,
        grid=(n_blocks,),
        in_specs=[
            pl.BlockSpec(memory_space=pl.ANY),
            pl.BlockSpec(memory_space=pl.ANY),
            pl.BlockSpec((1, d, de), lambda i, be, nu: (be[i], 0, 0)),
            pl.BlockSpec((1, d, de), lambda i, be, nu: (be[i], 0, 0)),
            pl.BlockSpec((1, de, d), lambda i, be, nu: (be[i], 0, 0)),
        ],
        out_specs=pl.BlockSpec((blk * SUBLANES, LANES), lambda i, be, nu: (i, 0)),
        scratch_shapes=[pltpu.SMEM((2, blk), I32), pltpu.VMEM((2, blk * SUBLANES, LANES), F32),
                        pltpu.SemaphoreType.DMA((2,)), pltpu.SemaphoreType.DMA((2,))],
    )
    return pl.pallas_call(
        _moe_kernel,
        grid_spec=grid_spec,
        out_shape=jax.ShapeDtypeStruct((n_blocks * blk * SUBLANES, LANES), F32),
        compiler_params=_params("arbitrary"),
        name="moe",
    )(block_expert, n_used, slot_row.reshape(n_blocks, blk), h3_tiles,
      w_gate.astype(BF16), w_up.astype(BF16), w_down.astype(BF16))


def _route_tables(eid, blk):
    n = eid.shape[0]
    n_assign = n * TOP_K
    flat_e = eid.reshape(-1)
    onehot = (flat_e[:, None] == jnp.arange(N_EXPERTS, dtype=I32)[None, :]).astype(I32)
    csum = jnp.cumsum(onehot, axis=0)
    rank = jnp.sum(csum * onehot, axis=1) - 1
    counts = csum[-1]
    padded = (counts + blk - 1) // blk * blk
    pad_end = jnp.cumsum(padded)
    dest = (pad_end - padded)[flat_e] + rank
    n_blocks = -(-n_assign // blk) + N_EXPERTS
    tok_row = jnp.arange(n_assign, dtype=I32) // TOP_K * SUBLANES
    slot_row = jnp.zeros((n_blocks * blk,), I32).at[dest].set(tok_row, unique_indices=True)
    block_start = jnp.arange(n_blocks, dtype=I32) * blk
    block_expert = jnp.minimum(jnp.sum((pad_end[None, :] <= block_start[:, None]).astype(I32), axis=1),
                               N_EXPERTS - 1)
    n_used = (pad_end[-1:] // blk).astype(I32)
    return slot_row, block_expert, n_used, (dest * SUBLANES).reshape(n, TOP_K).astype(I32)


def _combine_kernel(pos_hbm, x2_ref, gate_ref, nw_ref, ys_hbm, out_ref, pos_s, buf, isem, rsem):
    tm = x2_ref.shape[0]
    slot, drain = _gather_pipeline(pl.program_id(0), pl.num_programs(0), pos_hbm, pos_s, ys_hbm,
                                   buf, isem, rsem)
    drain()
    g = gate_ref[...]
    g0, g1 = g[:, 0:1], g[:, 1:2]
    y = jnp.concatenate(
        [g0 * buf[slot, _token_rows(j, tm), :] + g1 * buf[slot, _token_rows(tm * SUBLANES + j, tm), :]
         for j in range(SUBLANES)], axis=1)
    out_ref[...] = _rms(x2_ref[...] + y, nw_ref[...])


def _combine(x2, gates, ys_tiles, pos_row, norm_w, tm):
    n, d = x2.shape
    nt = n // tm
    pos_t = jnp.transpose(pos_row.reshape(nt, tm, TOP_K), (0, 2, 1)).reshape(nt, TOP_K * tm)
    return pl.pallas_call(
        _combine_kernel,
        grid=(nt,),
        in_specs=[
            pl.BlockSpec(memory_space=pl.ANY),
            pl.BlockSpec((tm, d), lambda i: (i, 0)),
            pl.BlockSpec((tm, ROUTER_LANES), lambda i: (i, 0)),
            pl.BlockSpec((1, d), lambda i: (0, 0)),
            pl.BlockSpec(memory_space=pl.ANY),
        ],
        out_specs=pl.BlockSpec((tm, d), lambda i: (i, 0)),
        out_shape=jax.ShapeDtypeStruct((n, d), F32),
        scratch_shapes=[pltpu.SMEM((2, TOP_K * tm), I32),
                        pltpu.VMEM((2, TOP_K * tm * SUBLANES, LANES), F32),
                        pltpu.SemaphoreType.DMA((2,)), pltpu.SemaphoreType.DMA((2,))],
        compiler_params=_params("arbitrary"),
        name="combine",
    )(pos_t, x2, gates, norm_w.reshape(1, d), ys_tiles)


def _tile(s, want):
    return min(s, want)


def kernel(x, mem, norm_mix_w, w_in, mu_shift, w0, w_decay_up, a0, w_iclr_up, w_gate_up, k_k, k_a, r_k, ln_x_w, ln_x_b, s5_lam_re, s5_lam_im, s5_log_dt, s5_b_re, s5_b_im, s5_c_re, s5_c_im, s5_d, w_glu, b_glu, w_mix_out, norm_xattn_w, norm_mem_w, w_q, w_k, w_v, w_o, norm_moe_w, w_router_group, b_router_group, w_router_expert, b_router_expert, w_exp_gate, w_exp_up, w_exp_down, norm_final_w):
    b, s, d = x.shape
    assert norm_mix_w.shape[0] == 1, "one layer"
    p, u = _inproj(x, norm_mix_w[0], w_in[0], mu_shift[0], _tile(s, 512))
    y_rwkv = _rwkv(p, w0[0], w_decay_up[0], a0[0], w_iclr_up[0], w_gate_up[0], k_k[0], k_a[0],
                   r_k[0], ln_x_w[0], ln_x_b[0], _tile(s, 256))
    y_s5 = _s5(u, _s5_tables(s5_lam_re[0], s5_lam_im[0], s5_log_dt[0], s5_b_re[0], s5_b_im[0],
                             s5_c_re[0], s5_c_im[0], s5_d[0]), _tile(s // S5_CHUNK * b, 512))
    kmem, vmem = _memkv(mem, norm_mem_w[0], w_k[0], w_v[0])
    x2, h3_tiles, eid, gates = _post(x, y_rwkv, y_s5, kmem, vmem, w_glu[0], b_glu[0], w_mix_out[0],
                                     norm_xattn_w[0], w_q[0], w_o[0], norm_moe_w[0],
                                     w_router_group[0], b_router_group[0], w_router_expert[0],
                                     b_router_expert[0], _tile(s, 256))
    n = b * s
    eid = eid.reshape(n, ROUTER_LANES)[:, :TOP_K]
    slot_row, block_expert, n_used, pos_row = _route_tables(eid, MOE_BLOCK_ROWS)
    ys_tiles = _moe(h3_tiles, slot_row, block_expert, n_used, w_exp_gate[0], w_exp_up[0], w_exp_down[0])
    out = _combine(x2.reshape(n, d), gates.reshape(n, ROUTER_LANES), ys_tiles, pos_row,
                   norm_final_w, _tile(n, 256))
    return out.reshape(b, s, d)
```

```python
import math

import jax
import jax.numpy as jnp
from jax import lax
from jax.experimental import pallas as pl
from jax.experimental.pallas import tpu as pltpu

F32 = jnp.float32
BF16 = jnp.bfloat16
I32 = jnp.int32

HEAD_DIM = 64
D_RWKV = 512
DECAY_RANK = 64
ICLR_RANK = 64
GATE_RANK = 128
D_SHIFT = 3 * D_RWKV + DECAY_RANK + ICLR_RANK + GATE_RANK
S5_CH = 16
S5_STATE = 64
D_S5 = 512
S5_GROUPS = D_S5 // S5_CH
XATTN_HEADS = 4
N_EXPERT_GROUPS = 4
EXPERTS_PER_GROUP = 8
N_EXPERTS = N_EXPERT_GROUPS * EXPERTS_PER_GROUP
TOP_K = 2
RMS_EPS = 1e-6
GN_EPS = 64e-5
L2_EPS = 1e-12

LANES = 128
SUBLANES = 8
VMEM_LIMIT_BYTES = 56 * 1024 * 1024

RWKV_CHUNK = 64
RWKV_CHUNKS_PER_ITER = 2
PAIR = 2 * HEAD_DIM
S5_CHUNK = 16
S5_OCT = LANES // S5_CH
S5_N_OCT = S5_GROUPS // S5_OCT
S5_ROW_W = S5_CHUNK * LANES
S5_STATE_W = S5_OCT * S5_STATE
MOE_BLOCK_ROWS = 256
ROUTER_LANES = 128


def _dot(a, b):
    return jnp.dot(a, b, preferred_element_type=F32)


def _dot_nt(a, b):
    return lax.dot_general(a, b, (((1,), (1,)), ((), ())), preferred_element_type=F32)


def _dot_tn(a, b):
    return lax.dot_general(a, b, (((0,), (0,)), ((), ())), preferred_element_type=F32)


def _split2(x):
    hi = x.astype(BF16)
    lo = (x - hi.astype(F32)).astype(BF16)
    return hi, lo


def _split3(x):
    hi = x.astype(BF16)
    r = x - hi.astype(F32)
    mid = r.astype(BF16)
    lo = (r - mid.astype(F32)).astype(BF16)
    return hi, mid, lo


def _stack3(w):
    hi, lo = _split2(w.astype(F32))
    return jnp.concatenate([hi, hi, lo], axis=0)


def _dot3(x, w3):
    hi, lo = _split2(x)
    return _dot(jnp.concatenate([hi, lo, hi], axis=1), w3)


def _rms(x, w):
    return x * lax.rsqrt(jnp.mean(x * x, axis=-1, keepdims=True) + RMS_EPS) * w


def _softplus(x):
    return jnp.maximum(x, 0.0) + jnp.log1p(jnp.exp(-jnp.abs(x)))


def _params(*sem):
    return pltpu.CompilerParams(dimension_semantics=sem, vmem_limit_bytes=VMEM_LIMIT_BYTES)


def _token_rows(j, n):
    return pl.ds(j, n, stride=SUBLANES)


def _inproj_kernel(x_ref, nw_ref, w_ref, mu_ref, p_ref, u_ref, carry_ref):
    tm = x_ref.shape[1]

    @pl.when(pl.program_id(1) == 0)
    def _():
        carry_ref[...] = jnp.zeros_like(carry_ref)

    h = _rms(x_ref[0], nw_ref[...]).astype(BF16)
    proj = _dot(h, w_ref[...])
    z = proj[:, :D_SHIFT]
    row = lax.broadcasted_iota(I32, z.shape, 0)
    prev = jnp.where(row == 0, carry_ref[0:1, :], pltpu.roll(z, 1, 0))
    carry_ref[0:1, :] = z[tm - 1:tm, :]
    p_ref[0] = z + (prev - z) * mu_ref[...]
    for o in range(S5_N_OCT):
        u = proj[:, D_SHIFT + o * LANES:D_SHIFT + (o + 1) * LANES].astype(BF16)
        u_ref[o] = u.reshape(tm // S5_CHUNK, S5_CHUNK, LANES)


def _inproj(x, norm_w, w_in, mu, tm):
    b, s, d = x.shape
    d_in = w_in.shape[1]
    assert d_in - D_SHIFT == D_S5
    return pl.pallas_call(
        _inproj_kernel,
        grid=(b, s // tm),
        in_specs=[
            pl.BlockSpec((1, tm, d), lambda i, j: (i, j, 0)),
            pl.BlockSpec((1, d), lambda i, j: (0, 0)),
            pl.BlockSpec((d, d_in), lambda i, j: (0, 0)),
            pl.BlockSpec((1, D_SHIFT), lambda i, j: (0, 0)),
        ],
        out_specs=[
            pl.BlockSpec((1, tm, D_SHIFT), lambda i, j: (i, j, 0)),
            pl.BlockSpec((S5_N_OCT, tm // S5_CHUNK, None, S5_CHUNK, LANES), lambda i, j: (0, j, i, 0, 0)),
        ],
        out_shape=[
            jax.ShapeDtypeStruct((b, s, D_SHIFT), F32),
            jax.ShapeDtypeStruct((S5_N_OCT, s // S5_CHUNK, b, S5_CHUNK, LANES), BF16),
        ],
        scratch_shapes=[pltpu.VMEM((SUBLANES, D_SHIFT), F32)],
        compiler_params=_params("arbitrary", "arbitrary"),
        name="inproj",
    )(x, norm_w.reshape(1, d), w_in.astype(BF16), mu.reshape(1, D_SHIFT))


def _headsum(x, ones2):
    hi, lo = _split2(x)
    return _dot(jnp.concatenate([hi, lo], axis=1), ones2)


def _rwkv_kernel(p_ref, wcat_ref, wg_ref, vec_ref, tri_ref, ones2_ref, y_ref,
                 lw_s, cum_s, a_s, g_s, h_s):
    t2 = p_ref.shape[1]
    c = RWKV_CHUNK
    n_pairs = D_RWKV // PAIR

    @pl.when(pl.program_id(1) == 0)
    def _():
        h_s[...] = jnp.zeros_like(h_s)

    lane_t = lax.broadcasted_iota(I32, (t2, LANES), 1)
    lo_col = 3 * D_RWKV
    x = p_ref[0, :, lo_col:lo_col + LANES]
    zz = _dot3(jnp.where(lane_t < DECAY_RANK, jnp.tanh(x), x), wcat_ref[...])
    lw = -jnp.exp(-_softplus(-(vec_ref[0:1, :] + zz[:, :D_RWKV])) - 0.5)
    lw_s[...] = lw
    hi, mid, lo = _split3(lw)
    c3 = _dot(tri_ref[...], jnp.concatenate([hi, mid, lo], axis=1))
    cum_s[...] = c3[:, :D_RWKV] + c3[:, D_RWKV:2 * D_RWKV] + c3[:, 2 * D_RWKV:]
    a_s[...] = jax.nn.sigmoid(vec_ref[1:2, :] + zz[:, D_RWKV:])
    dg = p_ref[0, :, lo_col + LANES:lo_col + 2 * LANES]
    g_s[...] = _dot(jax.nn.sigmoid(dg).astype(BF16), wg_ref[...])

    t_idx = lax.broadcasted_iota(I32, (c, PAIR), 0)
    lane = lax.broadcasted_iota(I32, (c, PAIR), 1)
    s_idx = lane & (HEAD_DIM - 1)
    mask_strict = s_idx < t_idx
    mask_incl = s_idx <= t_idx
    eye_lane = s_idx == t_idx
    head0 = lane < HEAD_DIM
    row_b = lax.broadcasted_iota(I32, (PAIR, PAIR), 0)
    col_b = lax.broadcasted_iota(I32, (PAIR, PAIR), 1)
    eye_block = row_b == col_b
    ones2 = ones2_ref[...]

    def block(v, dtype=BF16):
        return jnp.concatenate([jnp.where(head0, v, 0.0), jnp.where(head0, 0.0, v)], axis=0).astype(dtype)

    def cat0(a, b):
        return jnp.concatenate([a, b], axis=0)

    def body(it, carry):
        streams = [(ci, j) for ci in range(RWKV_CHUNKS_PER_ITER) for j in range(n_pairs)]
        ns = range(len(streams))
        rows_of = [pl.ds(pl.multiple_of((it * RWKV_CHUNKS_PER_ITER + ci) * c, c), c)
                   for ci in range(RWKV_CHUNKS_PER_ITER)]

        def ld(ref, s):
            ci, j = streams[s]
            return ref[rows_of[ci], j * PAIR:(j + 1) * PAIR]

        def ldp(s, off):
            ci, j = streams[s]
            return p_ref[0, rows_of[ci], off + j * PAIR:off + (j + 1) * PAIR]

        def vec(i, s):
            j = streams[s][1]
            return vec_ref[i:i + 1, j * PAIR:(j + 1) * PAIR]

        r = [ldp(s, 0) for s in ns]
        k = [ldp(s, D_RWKV) for s in ns]
        v = [ldp(s, 2 * D_RWKV) for s in ns]
        kk = [k[s] * vec(2, s) for s in ns]
        ss = [_headsum(kk[s] * kk[s], ones2) for s in ns]
        a_blk, v_blk, bk_end, bc_blk, km, w_end, r_t, lhs, rhs = [], [], [], [], [], [], [], [], []
        for s in ns:
            lw_c, cum, a = ld(lw_s, s), ld(cum_s, s), ld(a_s, s)
            kkn = kk[s] / jnp.maximum(jnp.sqrt(ss[s]), L2_EPS)
            km_s = k[s] * (1.0 + (a - 1.0) * vec(3, s))
            beta = kkn * a
            cum_end = cum[c - 1:c, :]
            e_neg = jnp.exp(-cum)
            e_end = jnp.exp(cum_end - cum)
            a_t = -kkn * jnp.exp(cum - lw_c)
            r_f = r[s] * jnp.exp(cum)
            bc = block(beta * e_end)
            km.append(km_s)
            w_end.append(jnp.exp(cum_end))
            r_t.append(r_f)
            a_blk.append(block(a_t))
            v_blk.append(block(v[s]))
            bc_blk.append(bc)
            bk_end.append(cat0(bc, block(km_s * e_end)))
            lhs.append(cat0(a_t, r_f).astype(BF16))
            rhs.append(cat0(block(beta * e_neg), block(km_s * e_neg)))

        gram = [_dot_nt(lhs[s], rhs[s]) for s in ns]
        l_ab = [jnp.where(mask_strict, gram[s][:c, :PAIR], 0.0) for s in ns]
        l_ak = [jnp.where(mask_strict, gram[s][:c, PAIR:], 0.0).astype(BF16) for s in ns]
        m_cat = [jnp.concatenate([jnp.where(mask_incl, gram[s][c:, :PAIR], 0.0),
                                  jnp.where(mask_incl, gram[s][c:, PAIR:], 0.0)],
                                 axis=1).astype(BF16) for s in ns]

        inv = [jnp.where(eye_lane, 1.0, l_ab[s]) for s in ns]
        lp = l_ab
        lp_blk = [block(lp[s]) for s in ns]
        n = 1
        while 2 * n < c:
            lp = [_dot(lp[s].astype(BF16), lp_blk[s]) for s in ns]
            lp_blk = [block(lp[s]) for s in ns]
            inv = [inv[s] + _dot(inv[s].astype(BF16), lp_blk[s]) for s in ns]
            n *= 2
        inv = [inv[s].astype(BF16) for s in ns]

        a_p = [block(_dot(inv[s], a_blk[s])) for s in ns]
        w1 = [block(_dot(l_ak[s], v_blk[s])) for s in ns]
        uv = [cat0(block(_dot(inv[s], w1[s])), v_blk[s]) for s in ns]
        r_p = [(r_t[s] + _dot(m_cat[s][:, :PAIR], a_p[s])).astype(BF16) for s in ns]
        y_p = [_dot(m_cat[s], uv[s]) for s in ns]
        g_t = [(jnp.where(eye_block, w_end[s], 0.0) + _dot_tn(a_p[s], bc_blk[s])).astype(BF16)
               for s in ns]
        h_p = [_dot_tn(uv[s], bk_end[s]) for s in ns]

        state = [h_s[j] for j in range(n_pairs)]
        y = [None] * len(streams)
        for ci in range(RWKV_CHUNKS_PER_ITER):
            base = ci * n_pairs
            for j in range(n_pairs):
                y[base + j] = _dot_nt(r_p[base + j], state[j].astype(BF16)) + y_p[base + j]
            s_lane = [(state[j][:c] + state[j][c:]).astype(BF16) for j in range(n_pairs)]
            state = [block(_dot(s_lane[j], g_t[base + j]), F32) + h_p[base + j]
                     for j in range(n_pairs)]
        for j in range(n_pairs):
            h_s[j] = state[j]

        mu = [_headsum(y[s], ones2) * (1.0 / HEAD_DIM) for s in ns]
        d = [y[s] - mu[s] for s in ns]
        var = [_headsum(d[s] * d[s], ones2) * (1.0 / HEAD_DIM) for s in ns]
        bonus = [_headsum(r[s] * km[s] * vec(4, s), ones2) for s in ns]
        for s in ns:
            ci, j = streams[s]
            yn = d[s] * lax.rsqrt(var[s] + GN_EPS) * vec(5, s) + vec(6, s)
            y_ref[0, rows_of[ci], j * PAIR:(j + 1) * PAIR] = (
                (yn + bonus[s] * v[s]) * ld(g_s, s)).astype(BF16)
        return carry

    lax.fori_loop(0, t2 // (c * RWKV_CHUNKS_PER_ITER), body, 0)


def _rwkv(p, w0, w_decay_up, a0, w_iclr_up, w_gate_up, k_k, k_a, r_k, ln_w, ln_b, t2):
    b, s, _ = p.shape
    zeros = jnp.zeros((DECAY_RANK, D_RWKV), F32)
    wcat = jnp.concatenate([jnp.concatenate([w_decay_up, zeros], axis=1),
                            jnp.concatenate([zeros, w_iclr_up], axis=1)], axis=0)
    vec = jnp.stack([w0, a0, k_k, k_a, r_k.reshape(-1), ln_w, ln_b, jnp.zeros_like(w0)])
    ti = jnp.arange(t2)
    tri = ((ti[:, None] // RWKV_CHUNK == ti[None, :] // RWKV_CHUNK) & (ti[None, :] <= ti[:, None])).astype(BF16)
    li = jnp.arange(PAIR)
    ones_bd = (li[:, None] // HEAD_DIM == li[None, :] // HEAD_DIM).astype(BF16)
    ones2 = jnp.concatenate([ones_bd, ones_bd], axis=0)
    const = lambda shape: pl.BlockSpec(shape, lambda i, j: (0,) * len(shape))
    return pl.pallas_call(
        _rwkv_kernel,
        grid=(b, s // t2),
        in_specs=[
            pl.BlockSpec((1, t2, D_SHIFT), lambda i, j: (i, j, 0)),
            const((3 * LANES, 2 * D_RWKV)),
            const((GATE_RANK, D_RWKV)),
            const((SUBLANES, D_RWKV)),
            const((t2, t2)),
            const((2 * PAIR, PAIR)),
        ],
        out_specs=pl.BlockSpec((1, t2, D_RWKV), lambda i, j: (i, j, 0)),
        out_shape=jax.ShapeDtypeStruct((b, s, D_RWKV), BF16),
        scratch_shapes=[pltpu.VMEM((t2, D_RWKV), F32)] * 4
        + [pltpu.VMEM((D_RWKV // PAIR, PAIR, PAIR), F32)],
        compiler_params=_params("arbitrary", "arbitrary"),
        name="rwkv",
    )(p, _stack3(wcat), w_gate_up.astype(BF16), vec, tri, ones2)


def _s5_kernel(u_ref, m_ref, pre_ref, pim_ref, qre_ref, qim_ref, lam_ref, y_ref, xr_s, xi_s, cr_s, ci_s):
    nb = SUBLANES

    @pl.when(pl.program_id(1) == 0)
    def _():
        cr_s[...] = jnp.zeros_like(cr_s)
        ci_s[...] = jnp.zeros_like(ci_s)

    u = u_ref[0]
    xr_s[...] = _dot(u, pre_ref[0])
    xi_s[...] = _dot(u, pim_ref[0])
    lr = jnp.broadcast_to(lam_ref[0, 0:1, :], (nb, S5_STATE_W))
    li = jnp.broadcast_to(lam_ref[0, 1:2, :], (nb, S5_STATE_W))

    def step(ci, carry):
        xr, xi = carry
        rows = pl.ds(pl.multiple_of(ci * nb, nb), nb)
        ar = xr_s[rows, :]
        ai = xi_s[rows, :]
        xr_s[rows, :] = xr
        xi_s[rows, :] = xi
        return lr * xr - li * xi + ar, lr * xi + li * xr + ai

    xr, xi = lax.fori_loop(0, u.shape[0] // nb, step, (cr_s[...], ci_s[...]), unroll=8)
    cr_s[...] = xr
    ci_s[...] = xi
    y = (_dot(u, m_ref[0]) + _dot(xr_s[...].astype(BF16), qre_ref[0])
         + _dot(xi_s[...].astype(BF16), qim_ref[0]))
    for t in range(S5_CHUNK):
        y_ref[0, pl.ds(t, u.shape[0], stride=S5_CHUNK), :] = y[:, t * LANES:(t + 1) * LANES]


def _s5_tables(lam_re, lam_im, log_dt, b_re, b_im, c_re, c_im, d_skip):
    tc = S5_CHUNK
    dt = jnp.exp(log_dt)[:, None]
    ar, ai = lam_re * dt, lam_im * dt
    taus = jnp.arange(tc + 1, dtype=F32)[:, None, None]
    mag = jnp.exp(ar[None] * taus)
    pw_re, pw_im = mag * jnp.cos(ai[None] * taus), mag * jnp.sin(ai[None] * taus)
    x_re, x_im = pw_re[1] - 1.0, pw_im[1]
    den = lam_re * lam_re + lam_im * lam_im
    co_re = (x_re * lam_re + x_im * lam_im) / den
    co_im = (x_im * lam_re - x_re * lam_im) / den
    bb_re = co_re[..., None] * b_re - co_im[..., None] * b_im
    bb_im = co_re[..., None] * b_im + co_im[..., None] * b_re
    w_re = pw_re[:tc, :, :, None] * bb_re[None] - pw_im[:tc, :, :, None] * bb_im[None]
    w_im = pw_re[:tc, :, :, None] * bb_im[None] + pw_im[:tc, :, :, None] * bb_re[None]
    kern = jnp.einsum('gjn,tgni->tgji', c_re, w_re) - jnp.einsum('gjn,tgni->tgji', c_im, w_im)
    kern = kern.at[0].add(d_skip[:, :, None] * jnp.eye(S5_CH, dtype=F32))
    eye = jnp.eye(S5_OCT, dtype=F32)
    kb = jnp.einsum('togji,gh->otgihj', kern.reshape(tc, S5_N_OCT, S5_OCT, S5_CH, S5_CH), eye)
    kb = kb.reshape(S5_N_OCT, tc, LANES, LANES)
    s_idx = jnp.arange(tc)
    cols = [jnp.where((t - s_idx >= 0)[None, :, None, None], kb[:, jnp.clip(t - s_idx, 0)], 0.0)
            for t in range(tc)]
    m8 = jnp.concatenate(cols, axis=-1).reshape(S5_N_OCT, S5_ROW_W, S5_ROW_W)
    flip = tc - 1 - jnp.arange(tc)

    def p_table(w):
        p6 = w[flip].reshape(tc, S5_N_OCT, S5_OCT, S5_STATE, S5_CH)
        return jnp.einsum('sogni,gh->osgihn', p6, eye).reshape(S5_N_OCT, S5_ROW_W, S5_STATE_W)

    q_re = c_re[None] * pw_re[1:, :, None, :] - c_im[None] * pw_im[1:, :, None, :]
    q_im = c_re[None] * pw_im[1:, :, None, :] + c_im[None] * pw_re[1:, :, None, :]

    def q_table(q):
        q6 = q.reshape(tc, S5_N_OCT, S5_OCT, S5_CH, S5_STATE)
        return jnp.einsum('togjn,gh->ognthj', q6, eye).reshape(S5_N_OCT, S5_STATE_W, S5_ROW_W)

    lam2 = jnp.stack([pw_re[tc].reshape(S5_N_OCT, S5_STATE_W), pw_im[tc].reshape(S5_N_OCT, S5_STATE_W)], axis=1)
    return (m8.astype(BF16), p_table(w_re).astype(BF16), p_table(w_im).astype(BF16),
            q_table(q_re).astype(BF16), q_table(-q_im).astype(BF16), lam2)


def _s5(u, tables, rt):
    n_oct, nc, b, _, _ = u.shape
    assert b == SUBLANES, "one block step of the state scan handles one sublane tile of batch rows"
    rows = nc * b
    m8, pre, pim, qre, qim, lam2 = tables
    per = lambda a, c_: pl.BlockSpec((1, a, c_), lambda o, i: (o, 0, 0))
    tile = pl.BlockSpec((1, rt, S5_ROW_W), lambda o, i: (o, i, 0))
    y = pl.pallas_call(
        _s5_kernel,
        grid=(n_oct, rows // rt),
        in_specs=[tile, per(S5_ROW_W, S5_ROW_W), per(S5_ROW_W, S5_STATE_W), per(S5_ROW_W, S5_STATE_W),
                  per(S5_STATE_W, S5_ROW_W), per(S5_STATE_W, S5_ROW_W), per(2, S5_STATE_W)],
        out_specs=pl.BlockSpec((1, rt * S5_CHUNK, LANES), lambda o, i: (o, i, 0)),
        out_shape=jax.ShapeDtypeStruct((n_oct, rows * S5_CHUNK, LANES), F32),
        scratch_shapes=[pltpu.VMEM((rt, S5_STATE_W), F32)] * 2 + [pltpu.VMEM((SUBLANES, S5_STATE_W), F32)] * 2,
        compiler_params=_params("arbitrary", "arbitrary"),
        name="s5",
    )(u.reshape(n_oct, rows, S5_ROW_W), m8, pre, pim, qre, qim, lam2)
    return y.reshape(n_oct, nc, b, S5_CHUNK, LANES)


def _memkv_kernel(mem_ref, nw_ref, wk_ref, wv_ref, k_ref, v_ref):
    m = _rms(mem_ref[0], nw_ref[...]).astype(BF16)
    k_ref[0] = _dot(m, wk_ref[...]).astype(BF16)
    v_ref[0] = _dot(m, wv_ref[...]).astype(BF16)


def _memkv(mem, norm_w, w_k, w_v):
    b, nm, d = mem.shape
    const = lambda shape: pl.BlockSpec(shape, lambda i: (0,) * len(shape))
    blk = pl.BlockSpec((1, nm, d), lambda i: (i, 0, 0))
    return pl.pallas_call(
        _memkv_kernel,
        grid=(b,),
        in_specs=[blk, const((1, d)), const((d, d)), const((d, d))],
        out_specs=[blk, blk],
        out_shape=[jax.ShapeDtypeStruct((b, nm, d), BF16)] * 2,
        compiler_params=_params("arbitrary"),
        name="memkv",
    )(mem, norm_w.reshape(1, d), w_k.astype(BF16), w_v.astype(BF16))


def _post_kernel(x_ref, yr_ref, y5_ref, km_ref, vm_ref, wglu_ref, bglu_ref, wmo_ref, nx_ref,
                 wq_ref, wo_ref, nm_ref, wr_ref, br_ref, x2_ref, h3_ref, eid_ref, gate_ref):
    tm, d = x_ref.shape[1], x_ref.shape[2]
    hd = d // XATTN_HEADS
    y = jnp.concatenate([y5_ref[o].reshape(tm, LANES) for o in range(S5_N_OCT)], axis=1)
    z = 0.5 * y * (1.0 + jnp.tanh(math.sqrt(2.0 / math.pi) * (y + 0.044715 * (y * y * y))))
    glu = jax.nn.sigmoid(_dot(z.astype(BF16), wglu_ref[...]) + bglu_ref[...])
    x1 = (x_ref[0] + _dot(yr_ref[0], wmo_ref[:D_RWKV, :])
          + _dot((z * glu).astype(BF16), wmo_ref[D_RWKV:, :]))

    h = _rms(x1, nx_ref[...]).astype(BF16)
    q = (_dot(h, wq_ref[...]) * (hd ** -0.5)).astype(BF16)
    heads = []
    for i in range(XATTN_HEADS):
        sl = slice(i * hd, (i + 1) * hd)
        sc = _dot_nt(q[:, sl], km_ref[0, :, sl])
        e = jnp.exp(sc - jnp.max(sc, axis=-1, keepdims=True))
        o = _dot(e.astype(BF16), vm_ref[0, :, sl]) / jnp.sum(e, axis=-1, keepdims=True)
        heads.append(o.astype(BF16))
    x2 = x1 + _dot(jnp.concatenate(heads, axis=1), wo_ref[...])
    x2_ref[0] = x2
    h3 = _rms(x2, nm_ref[...])
    for j in range(d // LANES):
        h3_ref[_token_rows(j, tm), :] = h3[:, j * LANES:(j + 1) * LANES]

    lg = _dot3(h3, wr_ref[...]) + br_ref[...]
    lane = lax.broadcasted_iota(I32, lg.shape, 1)
    lanef = lane.astype(F32)
    neg = -jnp.inf
    no_lane = float(ROUTER_LANES)
    l1 = jnp.where(lane < N_EXPERT_GROUPS, lg, neg)
    m1 = jnp.max(l1, axis=-1, keepdims=True)
    grp = jnp.min(jnp.where(l1 == m1, lanef, no_lane), axis=-1, keepdims=True)
    g1 = 1.0 / jnp.sum(jnp.exp(l1 - m1), axis=-1, keepdims=True)
    first = N_EXPERT_GROUPS + EXPERTS_PER_GROUP * grp
    l2 = jnp.where((lanef >= first) & (lanef < first + EXPERTS_PER_GROUP), lg, neg)
    v1 = jnp.max(l2, axis=-1, keepdims=True)
    i1 = jnp.min(jnp.where(l2 == v1, lanef, no_lane), axis=-1, keepdims=True)
    l2 = jnp.where(lanef == i1, neg, l2)
    v2 = jnp.max(l2, axis=-1, keepdims=True)
    i2 = jnp.min(jnp.where(l2 == v2, lanef, no_lane), axis=-1, keepdims=True)
    e2 = jnp.exp(v2 - v1)
    den = 1.0 + e2
    eid_ref[0] = jnp.where(lane == 0, i1 - N_EXPERT_GROUPS,
                           jnp.where(lane == 1, i2 - N_EXPERT_GROUPS, 0.0)).astype(I32)
    gate_ref[0] = jnp.where(lane == 0, g1 / den, jnp.where(lane == 1, g1 * e2 / den, 0.0))


def _post(x, y_rwkv, y_s5, kmem, vmem, w_glu, b_glu, w_mix_out, norm_x, w_q, w_o, norm_moe,
          w_rg, b_rg, w_re, b_re, tm):
    b, s, d = x.shape
    assert d % LANES == 0 and d // LANES == SUBLANES, "one (8, 128) tile per token"
    nm = kmem.shape[1]
    nt = s // tm
    n_logits = N_EXPERT_GROUPS + N_EXPERTS
    w_router = jnp.concatenate(
        [w_rg, jnp.transpose(w_re, (1, 0, 2)).reshape(d, N_EXPERTS),
         jnp.zeros((d, ROUTER_LANES - n_logits), F32)], axis=1)
    b_router = jnp.concatenate([b_rg, b_re.reshape(-1), jnp.zeros((ROUTER_LANES - n_logits,), F32)])
    const = lambda shape: pl.BlockSpec(shape, lambda i, j: (0,) * len(shape))
    tok = lambda w: pl.BlockSpec((1, tm, w), lambda i, j: (i, j, 0))
    mem = pl.BlockSpec((1, nm, d), lambda i, j: (i, 0, 0))
    s5_tile = pl.BlockSpec((S5_N_OCT, tm // S5_CHUNK, None, S5_CHUNK, LANES), lambda i, j: (0, j, i, 0, 0))
    return pl.pallas_call(
        _post_kernel,
        grid=(b, nt),
        in_specs=[tok(d), tok(D_RWKV), s5_tile, mem, mem,
                  const((D_S5, D_S5)), const((1, D_S5)), const((D_RWKV + D_S5, d)), const((1, d)),
                  const((d, d)), const((d, d)), const((1, d)),
                  const((3 * d, ROUTER_LANES)), const((1, ROUTER_LANES))],
        out_specs=[tok(d), pl.BlockSpec((tm * SUBLANES, LANES), lambda i, j: (i * nt + j, 0)),
                   tok(ROUTER_LANES), tok(ROUTER_LANES)],
        out_shape=[jax.ShapeDtypeStruct((b, s, d), F32),
                   jax.ShapeDtypeStruct((b * s * SUBLANES, LANES), F32),
                   jax.ShapeDtypeStruct((b, s, ROUTER_LANES), I32),
                   jax.ShapeDtypeStruct((b, s, ROUTER_LANES), F32)],
        compiler_params=_params("arbitrary", "arbitrary"),
        name="post",
    )(x, y_rwkv, y_s5, kmem, vmem, w_glu.astype(BF16), b_glu.reshape(1, D_S5),
      w_mix_out.astype(BF16), norm_x.reshape(1, d), w_q.astype(BF16), w_o.astype(BF16),
      norm_moe.reshape(1, d), _stack3(w_router), b_router.reshape(1, ROUTER_LANES))


def _gather_pipeline(step, n_steps, idx_hbm, idx_s, src_hbm, buf, isem, rsem):
    n_tok = buf.shape[1] // SUBLANES
    slot = step % 2

    def idx_copy(t, s):
        return pltpu.make_async_copy(idx_hbm.at[t], idx_s.at[s], isem.at[s])

    def tile_copy(src_row, s, r):
        if not isinstance(src_row, int):
            src_row = pl.multiple_of(src_row, SUBLANES)
        return pltpu.make_async_copy(src_hbm.at[pl.ds(src_row, SUBLANES)],
                                     buf.at[s, pl.ds(r * SUBLANES, SUBLANES)], rsem.at[s])

    def start_tiles(s):
        for r in range(n_tok):
            tile_copy(idx_s[s, r], s, r).start()

    @pl.when(step == 0)
    def _():
        idx_copy(0, 0).start()
        idx_copy(0, 0).wait()
        start_tiles(0)

        @pl.when(n_steps > 1)
        def _():
            idx_copy(1, 1).start()

    @pl.when(step + 1 < n_steps)
    def _():
        idx_copy(step + 1, 1 - slot).wait()
        start_tiles(1 - slot)

    @pl.when(step + 2 < n_steps)
    def _():
        idx_copy(step + 2, slot).start()

    def drain():
        for r in range(n_tok):
            tile_copy(0, slot, r).wait()

    return slot, drain


def _moe_kernel(be_ref, nu_ref, tok_hbm, h_hbm, wg_ref, wu_ref, wd_ref, ys_ref,
                tok_s, buf, isem, rsem):
    del be_ref
    b = pl.program_id(0)
    n_used = nu_ref[0]
    blk = buf.shape[1] // SUBLANES
    slot, drain = _gather_pipeline(b, n_used, tok_hbm, tok_s, h_hbm, buf, isem, rsem)

    @pl.when(b < n_used)
    def _():
        drain()
        xb = jnp.concatenate([buf[slot, _token_rows(j, blk), :].astype(BF16) for j in range(SUBLANES)],
                             axis=1)
        gate = _dot(xb, wg_ref[0])
        hid = gate * jax.nn.sigmoid(gate) * _dot(xb, wu_ref[0])
        y = _dot(hid.astype(BF16), wd_ref[0])
        for j in range(SUBLANES):
            ys_ref[_token_rows(j, blk), :] = y[:, j * LANES:(j + 1) * LANES]

    @pl.when(b >= n_used)
    def _():
        ys_ref[...] = jnp.zeros_like(ys_ref)


def _moe(h3_tiles, slot_row, block_expert, n_used, w_gate, w_up, w_down):
    blk = MOE_BLOCK_ROWS
    n_blocks = slot_row.shape[0] // blk
    d, de = w_gate.shape[1], w_gate.shape[2]
    grid_spec = pltpu.PrefetchScalarGridSpec(
        num_scalar_prefetch=2,
        grid=(n_blocks,),
        in_specs=[
            pl.BlockSpec(memory_space=pl.ANY),
            pl.BlockSpec(memory_space=pl.ANY),
            pl.BlockSpec((1, d, de), lambda i, be, nu: (be[i], 0, 0)),
            pl.BlockSpec((1, d, de), lambda i, be, nu: (be[i], 0, 0)),
            pl.BlockSpec((1, de, d), lambda i, be, nu: (be[i], 0, 0)),
        ],
        out_specs=pl.BlockSpec((blk * SUBLANES, LANES), lambda i, be, nu: (i, 0)),
        scratch_shapes=[pltpu.SMEM((2, blk), I32), pltpu.VMEM((2, blk * SUBLANES, LANES), F32),
                        pltpu.SemaphoreType.DMA((2,)), pltpu.SemaphoreType.DMA((2,))],
    )
    return pl.pallas_call(
        _moe_kernel,
        grid_spec=grid_spec,
        out_shape=jax.ShapeDtypeStruct((n_blocks * blk * SUBLANES, LANES), F32),
        compiler_params=_params("arbitrary"),
        name="moe",
    )(block_expert, n_used, slot_row.reshape(n_blocks, blk), h3_tiles,
      w_gate.astype(BF16), w_up.astype(BF16), w_down.astype(BF16))


def _route_tables(eid, blk):
    n = eid.shape[0]
    n_assign = n * TOP_K
    flat_e = eid.reshape(-1)
    onehot = (flat_e[:, None] == jnp.arange(N_EXPERTS, dtype=I32)[None, :]).astype(I32)
    csum = jnp.cumsum(onehot, axis=0)
    rank = jnp.sum(csum * onehot, axis=1) - 1
    counts = csum[-1]
    padded = (counts + blk - 1) // blk * blk
    pad_end = jnp.cumsum(padded)
    dest = (pad_end - padded)[flat_e] + rank
    n_blocks = -(-n_assign // blk) + N_EXPERTS
    tok_row = jnp.arange(n_assign, dtype=I32) // TOP_K * SUBLANES
    slot_row = jnp.zeros((n_blocks * blk,), I32).at[dest].set(tok_row, unique_indices=True)
    block_start = jnp.arange(n_blocks, dtype=I32) * blk
    block_expert = jnp.minimum(jnp.sum((pad_end[None, :] <= block_start[:, None]).astype(I32), axis=1),
                               N_EXPERTS - 1)
    n_used = (pad_end[-1:] // blk).astype(I32)
    return slot_row, block_expert, n_used, (dest * SUBLANES).reshape(n, TOP_K).astype(I32)


def _combine_kernel(pos_hbm, x2_ref, gate_ref, nw_ref, ys_hbm, out_ref, pos_s, buf, isem, rsem):
    tm = x2_ref.shape[0]
    slot, drain = _gather_pipeline(pl.program_id(0), pl.num_programs(0), pos_hbm, pos_s, ys_hbm,
                                   buf, isem, rsem)
    drain()
    g = gate_ref[...]
    g0, g1 = g[:, 0:1], g[:, 1:2]
    y = jnp.concatenate(
        [g0 * buf[slot, _token_rows(j, tm), :] + g1 * buf[slot, _token_rows(tm * SUBLANES + j, tm), :]
         for j in range(SUBLANES)], axis=1)
    out_ref[...] = _rms(x2_ref[...] + y, nw_ref[...])


def _combine(x2, gates, ys_tiles, pos_row, norm_w, tm):
    n, d = x2.shape
    nt = n // tm
    pos_t = jnp.transpose(pos_row.reshape(nt, tm, TOP_K), (0, 2, 1)).reshape(nt, TOP_K * tm)
    return pl.pallas_call(
        _combine_kernel,
        grid=(nt,),
        in_specs=[
            pl.BlockSpec(memory_space=pl.ANY),
            pl.BlockSpec((tm, d), lambda i: (i, 0)),
            pl.BlockSpec((tm, ROUTER_LANES), lambda i: (i, 0)),
            pl.BlockSpec((1, d), lambda i: (0, 0)),
            pl.BlockSpec(memory_space=pl.ANY),
        ],
        out_specs=pl.BlockSpec((tm, d), lambda i: (i, 0)),
        out_shape=jax.ShapeDtypeStruct((n, d), F32),
        scratch_shapes=[pltpu.SMEM((2, TOP_K * tm), I32),
                        pltpu.VMEM((2, TOP_K * tm * SUBLANES, LANES), F32),
                        pltpu.SemaphoreType.DMA((2,)), pltpu.SemaphoreType.DMA((2,))],
        compiler_params=_params("arbitrary"),
        name="combine",
    )(pos_t, x2, gates, norm_w.reshape(1, d), ys_tiles)


def _tile(s, want):
    return min(s, want)


def kernel(x, mem, norm_mix_w, w_in, mu_shift, w0, w_decay_up, a0, w_iclr_up, w_gate_up, k_k, k_a, r_k, ln_x_w, ln_x_b, s5_lam_re, s5_lam_im, s5_log_dt, s5_b_re, s5_b_im, s5_c_re, s5_c_im, s5_d, w_glu, b_glu, w_mix_out, norm_xattn_w, norm_mem_w, w_q, w_k, w_v, w_o, norm_moe_w, w_router_group, b_router_group, w_router_expert, b_router_expert, w_exp_gate, w_exp_up, w_exp_down, norm_final_w):
    b, s, d = x.shape
    assert norm_mix_w.shape[0] == 1, "one layer"
    p, u = _inproj(x, norm_mix_w[0], w_in[0], mu_shift[0], _tile(s, 512))
    y_rwkv = _rwkv(p, w0[0], w_decay_up[0], a0[0], w_iclr_up[0], w_gate_up[0], k_k[0], k_a[0],
                   r_k[0], ln_x_w[0], ln_x_b[0], _tile(s, 256))
    y_s5 = _s5(u, _s5_tables(s5_lam_re[0], s5_lam_im[0], s5_log_dt[0], s5_b_re[0], s5_b_im[0],
                             s5_c_re[0], s5_c_im[0], s5_d[0]), _tile(s // S5_CHUNK * b, 512))
    kmem, vmem = _memkv(mem, norm_mem_w[0], w_k[0], w_v[0])
    x2, h3_tiles, eid, gates = _post(x, y_rwkv, y_s5, kmem, vmem, w_glu[0], b_glu[0], w_mix_out[0],
                                     norm_xattn_w[0], w_q[0], w_o[0], norm_moe_w[0],
                                     w_router_group[0], b_router_group[0], w_router_expert[0],
                                     b_router_expert[0], _tile(s, 256))
    n = b * s
    eid = eid.reshape(n, ROUTER_LANES)[:, :TOP_K]
    slot_row, block_expert, n_used, pos_row = _route_tables(eid, MOE_BLOCK_ROWS)
    ys_tiles = _moe(h3_tiles, slot_row, block_expert, n_used, w_exp_gate[0], w_exp_up[0], w_exp_down[0])
    out = _combine(x2.reshape(n, d), gates.reshape(n, ROUTER_LANES), ys_tiles, pos_row,
                   norm_final_w, _tile(n, 256))
    return out.reshape(b, s, d)
```
